```python
import math
import jax, jax.numpy as jnp
from jax import lax
import numpy as np

D_MODEL = 1024
BATCH = 8
SEQ = 2048
DEPTH = 2

CHUNK = 64
Q_BLOCK = 128
N_MIXERS = 2
DA_HEAD_DIM = 64
DA_HEADS = D_MODEL // (2 * DA_HEAD_DIM)
SB_HEAD_DIM = 64
SB_HEADS = D_MODEL // SB_HEAD_DIM
D_FF = 2816
PLE_DIM = 256
N_NORMS = 8
EPS = 1e-6
NEG_BIG = -1e30

kernel_name = "hybrid_diffattn_stickbreaking_macaron_trunk"


def rmsnorm(x, g):
    xf = x.astype(jnp.float32)
    y = xf * lax.rsqrt(jnp.mean(xf * xf, axis=-1, keepdims=True) + EPS)
    return (y * g.astype(jnp.float32)).astype(x.dtype)


def swiglu(h, w_in, w_out):
    a, b = jnp.split(h @ w_in, 2, axis=-1)
    return (jax.nn.silu(a) * b) @ w_out


def alibi_slopes(n_heads):
    return jnp.asarray([2.0 ** (-8.0 * (h + 1) / n_heads) for h in range(n_heads)], dtype=jnp.float32)


def diff_attention(h, w_qkv, w_o, lam_vecs, subln_g, lambda_init):
    B, S, _ = h.shape
    H, d = DA_HEADS, DA_HEAD_DIM
    q, k, v = jnp.split(h @ w_qkv, 3, axis=-1)
    q = q.reshape(B, S, H, 2, d)
    k = k.reshape(B, S, H, 2, d)
    v = v.reshape(B, S, H, 2 * d)
    lv = lam_vecs.astype(jnp.float32)
    lam = jnp.exp(jnp.sum(lv[0] * lv[1])) - jnp.exp(jnp.sum(lv[2] * lv[3])) + lambda_init
    slopes = alibi_slopes(H)
    scale = 1.0 / math.sqrt(d)
    outs = []
    for blk in range(S // Q_BLOCK):
        q0, end = blk * Q_BLOCK, (blk + 1) * Q_BLOCK
        qb, kb, vb = q[:, q0:end], k[:, :end], v[:, :end]
        s = jnp.einsum('bqhcd,bkhcd->bchqk', qb, kb).astype(jnp.float32) * scale
        tq = jnp.arange(q0, end)
        tk = jnp.arange(end)
        dist = jnp.abs(tq[:, None] - tk[None, :]).astype(jnp.float32)
        bias = -slopes[:, None, None] * dist
        allowed = (tk // CHUNK)[None, :] <= (tq // CHUNK)[:, None]
        s = jnp.where(allowed, s + bias, NEG_BIG)
        pr = jax.nn.softmax(s, axis=-1)
        a = pr[:, 0] - lam * pr[:, 1]
        outs.append(jnp.einsum('bhqk,bkhe->bqhe', a.astype(vb.dtype), vb))
    o = jnp.concatenate(outs, axis=1)
    o = rmsnorm(o, subln_g) * (1.0 - lambda_init)
    return o.reshape(B, S, H * 2 * d) @ w_o


def stick_breaking_attention(h, w_qkv, w_o):
    B, S, _ = h.shape
    H, d = SB_HEADS, SB_HEAD_DIM
    q, k, v = jnp.split(h @ w_qkv, 3, axis=-1)
    q = q.reshape(B, S, H, d)
    k = k.reshape(B, S, H, d)
    v = v.reshape(B, S, H, d)
    scale = 1.0 / math.sqrt(d)
    outs = []
    for blk in range(S // Q_BLOCK):
        q0, end = blk * Q_BLOCK, (blk + 1) * Q_BLOCK
        qb, kb, vb = q[:, q0:end], k[:, :end], v[:, :end]
        z = jnp.einsum('bqhd,bkhd->bhqk', qb, kb).astype(jnp.float32) * scale
        tq = jnp.arange(q0, end)
        tk = jnp.arange(end)
        strict = tk[None, :] < tq[:, None]
        log_not = jnp.where(strict, jax.nn.log_sigmoid(-z), 0.0)
        suffix = lax.cumsum(log_not, axis=3, reverse=True) - log_not
        log_a = jax.nn.log_sigmoid(z) + suffix
        a = jnp.where(strict, jnp.exp(log_a), 0.0)
        outs.append(jnp.einsum('bhqk,bkhd->bqhd', a.astype(vb.dtype), vb))
    o = jnp.concatenate(outs, axis=1)
    return o.reshape(B, S, H * d) @ w_o


def setup_inputs(seed: int = 0) -> dict:
    key = jax.random.key(seed)
    ks = jax.random.split(key, 16)
    n_a = (DEPTH + 1) // 2
    n_b = DEPTH // 2
    D, F = D_MODEL, D_FF
    nrm = jax.random.normal
    x = nrm(ks[0], (BATCH, SEQ, D), jnp.float32)
    p = nrm(ks[1], (DEPTH, BATCH, SEQ, PLE_DIM), jnp.float32)
    norm_gains = 1.0 + 0.05 * nrm(ks[2], (DEPTH, N_NORMS, D), jnp.float32)
    ffn1_w_in = nrm(ks[3], (DEPTH, D, 2 * F), jnp.float32) * D ** -0.5
    ffn1_w_out = nrm(ks[4], (DEPTH, F, D), jnp.float32) * F ** -0.5
    ffn2_w_in = nrm(ks[5], (DEPTH, D, 2 * F), jnp.float32) * D ** -0.5
    ffn2_w_out = nrm(ks[6], (DEPTH, F, D), jnp.float32) * F ** -0.5
    da_w_qkv = nrm(ks[7], (n_a, D, 3 * D), jnp.float32) * D ** -0.5
    da_w_o = nrm(ks[8], (n_a, D, D), jnp.float32) * D ** -0.5
    da_lambda = 0.1 * nrm(ks[9], (n_a, 4, DA_HEAD_DIM), jnp.float32)
    da_subln = 1.0 + 0.05 * nrm(ks[10], (n_a, 2 * DA_HEAD_DIM), jnp.float32)
    sb_w_qkv = nrm(ks[11], (n_b, D, 3 * D), jnp.float32) * D ** -0.5
    sb_w_o = nrm(ks[12], (n_b, D, D), jnp.float32) * D ** -0.5
    ple_w_proj = nrm(ks[13], (DEPTH, PLE_DIM, D), jnp.float32) * PLE_DIM ** -0.5
    ple_w_gate = nrm(ks[14], (DEPTH, D, D), jnp.float32) * D ** -0.5
    return {"x": x, "p": p, "norm_gains": norm_gains,
            "ffn1_w_in": ffn1_w_in, "ffn1_w_out": ffn1_w_out,
            "ffn2_w_in": ffn2_w_in, "ffn2_w_out": ffn2_w_out,
            "da_w_qkv": da_w_qkv, "da_w_o": da_w_o, "da_lambda": da_lambda, "da_subln": da_subln,
            "sb_w_qkv": sb_w_qkv, "sb_w_o": sb_w_o,
            "ple_w_proj": ple_w_proj, "ple_w_gate": ple_w_gate}


def reference(x, p, norm_gains, ffn1_w_in, ffn1_w_out, ffn2_w_in, ffn2_w_out,
              da_w_qkv, da_w_o, da_lambda, da_subln, sb_w_qkv, sb_w_o,
              ple_w_proj, ple_w_gate):
    for i in range(DEPTH):
        g = norm_gains[i]
        x = x + 0.5 * rmsnorm(swiglu(rmsnorm(x, g[0]), ffn1_w_in[i], ffn1_w_out[i]), g[1])
        h = rmsnorm(x, g[2])
        j = i // N_MIXERS
        if i % N_MIXERS == 0:
            lambda_init = 0.8 - 0.6 * math.exp(-0.3 * i)
            m = diff_attention(h, da_w_qkv[j], da_w_o[j], da_lambda[j], da_subln[j], lambda_init)
        else:
            m = stick_breaking_attention(h, sb_w_qkv[j], sb_w_o[j])
        x = x + rmsnorm(m, g[3])
        x = x + 0.5 * rmsnorm(swiglu(rmsnorm(x, g[4]), ffn2_w_in[i], ffn2_w_out[i]), g[5])
        e = p[i] @ ple_w_proj[i]
        gate = jax.nn.sigmoid(rmsnorm(x, g[6]) @ ple_w_gate[i])
        x = x + rmsnorm(gate * e, g[7])
    return x
```

```python
import functools
import math

import jax
import jax.numpy as jnp
from jax import lax
from jax.experimental import pallas as pl
from jax.experimental.pallas import tpu as pltpu

EPS = 1e-6
NEG_BIG = -1e30
CHUNK = 64
HEAD_LANES = 128
HALF = 64
QK_SCALE = 1.0 / math.sqrt(HALF)
ATT_BLOCK = 256
ROW_TILE = 512
FF_CHUNK = 256
QKV_CHUNK = 512
VMEM_LIMIT = 56 * 1024 * 1024

F32 = jnp.float32
BF16 = jnp.bfloat16


def _rms(x, g):
    ms = jnp.mean(x * x, axis=-1, keepdims=True)
    return x * lax.rsqrt(ms + EPS) * g


def _dot(a, b):
    return jnp.dot(a, b, preferred_element_type=F32)


def _dot_nt(a, b):
    return lax.dot_general(a, b, (((1,), (1,)), ((), ())), preferred_element_type=F32)


def _resident(shape):
    zeros = (0,) * len(shape)
    return pl.BlockSpec(shape, lambda *_: zeros, pipeline_mode=pl.Buffered(1))


def _params(n_axes):
    return pltpu.CompilerParams(
        dimension_semantics=("arbitrary",) * n_axes, vmem_limit_bytes=VMEM_LIMIT)


def _ffn_kernel(x_ref, gpre_ref, gpost_ref, win_ref, wout_ref, o_ref, h_ref, acc_ref):
    x = x_ref[...]
    h_ref[...] = _rms(x, gpre_ref[...]).astype(BF16)
    acc_ref[...] = jnp.zeros_like(acc_ref)

    def body(j, carry):
        h = h_ref[...]
        a = _dot(h, win_ref[0, j])
        b = _dot(h, win_ref[1, j])
        g = (a * jax.nn.sigmoid(a) * b).astype(BF16)
        acc_ref[...] += _dot(g, wout_ref[j])
        return carry

    lax.fori_loop(0, win_ref.shape[1], body, 0)
    o_ref[...] = x + 0.5 * _rms(acc_ref[...], gpost_ref[...])


def _ffn(x, g_pre, g_post, w_in, w_out):
    n, d = x.shape
    f = w_out.shape[0]
    nf = f // FF_CHUNK
    win = w_in.reshape(d, 2, nf, FF_CHUNK).transpose(1, 2, 0, 3).astype(BF16)
    wout = w_out.reshape(nf, FF_CHUNK, d).astype(BF16)
    row = pl.BlockSpec((ROW_TILE, d), lambda i: (i, 0))
    return pl.pallas_call(
        _ffn_kernel,
        grid=(n // ROW_TILE,),
        in_specs=[row, _resident((1, d)), _resident((1, d)),
                  _resident(win.shape), _resident(wout.shape)],
        out_specs=row,
        out_shape=jax.ShapeDtypeStruct((n, d), F32),
        scratch_shapes=[pltpu.VMEM((ROW_TILE, d), BF16), pltpu.VMEM((ROW_TILE, d), F32)],
        compiler_params=_params(1),
        name="ffn",
    )(x, g_pre.reshape(1, d), g_post.reshape(1, d), win, wout)


def _qkv_kernel(x_ref, g_ref, w_ref, o_ref):
    h = _rms(x_ref[...], g_ref[...]).astype(BF16)
    for c in range(w_ref.shape[1] // QKV_CHUNK):
        cols = slice(c * QKV_CHUNK, (c + 1) * QKV_CHUNK)
        o_ref[:, cols] = _dot(h, w_ref[:, cols]).astype(BF16)


def _qkv_proj(x, g, w_qkv):
    n, d = x.shape
    d3 = w_qkv.shape[1]
    return pl.pallas_call(
        _qkv_kernel,
        grid=(n // ROW_TILE,),
        in_specs=[pl.BlockSpec((ROW_TILE, d), lambda i: (i, 0)),
                  _resident((1, d)), _resident((d, d3))],
        out_specs=pl.BlockSpec((ROW_TILE, d3), lambda i: (i, 0)),
        out_shape=jax.ShapeDtypeStruct((n, d3), BF16),
        compiler_params=_params(1),
        name="qkv_proj",
    )(x, g.reshape(1, d), w_qkv.astype(BF16))


def _stack_halves(q):
    lane = lax.broadcasted_iota(jnp.int32, q.shape, 1)
    qs = q.astype(F32) * QK_SCALE
    lo = jnp.where(lane < HALF, qs, 0.0)
    hi = jnp.where(lane >= HALF, qs, 0.0)
    return jnp.concatenate([lo, hi], axis=0).astype(BF16)


def _attn_specs(d):
    nh = d // HEAD_LANES

    def spec(off, seq):
        return pl.BlockSpec((1, seq, HEAD_LANES), lambda b, h, *_: (b, 0, off * nh + h))
    return nh, spec


def _da_kernel(slope_ref, lamv_ref, subln_ref, q_ref, k_ref, v_ref, o_ref, *, lambda_init):
    t = ATT_BLOCK
    n_blocks = q_ref.shape[1] // t
    slope = slope_ref[pl.program_id(1)]

    lv = lamv_ref[...]
    lam = (jnp.exp(jnp.sum(lv[0:1] * lv[1:2], axis=1, keepdims=True))
           - jnp.exp(jnp.sum(lv[2:3] * lv[3:4], axis=1, keepdims=True)) + lambda_init)
    subln = subln_ref[...]

    row = lax.broadcasted_iota(jnp.int32, (2 * t, t), 0) % t
    col = lax.broadcasted_iota(jnp.int32, (2 * t, t), 1)
    rel = (row - col).astype(F32)
    off_bias = -slope * rel
    diag_bias = jnp.where(col // CHUNK <= row // CHUNK, -slope * jnp.abs(rel), NEG_BIG)

    def update(q2, j, bias, carry):
        m, l, acc = carry
        k = k_ref[0, pl.ds(pl.multiple_of(j * t, t), t), :]
        v = v_ref[0, pl.ds(pl.multiple_of(j * t, t), t), :]
        s = _dot_nt(q2, k) + bias
        m_new = jnp.maximum(m, jnp.max(s, axis=1, keepdims=True))
        alpha = jnp.exp(m - m_new)
        p = jnp.exp(s - m_new)
        l = alpha * l + jnp.sum(p, axis=1, keepdims=True)
        acc = alpha * acc + _dot(p.astype(BF16), v)
        return m_new, l, acc

    def q_block(i, carry):
        rows = pl.ds(pl.multiple_of(i * t, t), t)
        q2 = _stack_halves(q_ref[0, rows, :])

        def k_block(j, c):
            shift = -slope * ((i - j) * t).astype(F32)
            return update(q2, j, off_bias + shift, c)

        init = (jnp.full((2 * t, 1), NEG_BIG, F32), jnp.zeros((2 * t, 1), F32),
                jnp.zeros((2 * t, HEAD_LANES), F32))
        c = lax.fori_loop(0, i, k_block, init)
        _, l, acc = update(q2, i, diag_bias, c)
        o = acc[:t] / l[:t] - lam * (acc[t:] / l[t:])
        o = _rms(o, subln) * (1.0 - lambda_init)
        o_ref[0, rows, :] = o.astype(BF16)
        return carry

    lax.fori_loop(0, n_blocks, q_block, 0)


def _diff_attention(qkv, lam_vecs, subln_g, lambda_init):
    b, s, d3 = qkv.shape
    d = d3 // 3
    nh, spec = _attn_specs(d)
    slopes = jnp.asarray([2.0 ** (-8.0 * (h + 1) / nh) for h in range(nh)], F32)
    grid_spec = pltpu.PrefetchScalarGridSpec(
        num_scalar_prefetch=1,
        grid=(b, nh),
        in_specs=[pl.BlockSpec(lam_vecs.shape, lambda *_: (0, 0)),
                  pl.BlockSpec((1, HEAD_LANES), lambda *_: (0, 0)),
                  spec(0, s), spec(1, s), spec(2, s)],
        out_specs=pl.BlockSpec((1, s, HEAD_LANES), lambda bi, h, *_: (bi, 0, h)),
    )
    return pl.pallas_call(
        functools.partial(_da_kernel, lambda_init=lambda_init),
        grid_spec=grid_spec,
        out_shape=jax.ShapeDtypeStruct((b, s, d), BF16),
        compiler_params=_params(2),
        name="diff_attention",
    )(slopes, lam_vecs, subln_g.reshape(1, HEAD_LANES), qkv, qkv, qkv)


def _sb_kernel(q_ref, k_ref, v_ref, o_ref):
    t = ATT_BLOCK
    n_blocks = q_ref.shape[1] // t

    row = lax.broadcasted_iota(jnp.int32, (2 * t, t), 0) % t
    col = lax.broadcasted_iota(jnp.int32, (2 * t, t), 1)
    strict = col < row
    kr = lax.broadcasted_iota(jnp.int32, (t, t), 0)
    kc = lax.broadcasted_iota(jnp.int32, (t, t), 1)
    later = (kr > kc).astype(BF16)
    lane = lax.broadcasted_iota(jnp.int32, (t, HEAD_LANES), 1)

    def update(q2, j, masked, carry):
        tail, acc = carry
        k = k_ref[0, pl.ds(pl.multiple_of(j * t, t), t), :]
        v = v_ref[0, pl.ds(pl.multiple_of(j * t, t), t), :]
        z = _dot_nt(q2, k)
        log_beta = jnp.minimum(z, 0.0) - jnp.log(1.0 + jnp.exp(-jnp.abs(z)))
        log_not = log_beta - z
        if masked:
            log_not = jnp.where(strict, log_not, 0.0)
        hi = log_not.astype(BF16)
        lo = (log_not - hi.astype(F32)).astype(BF16)
        suffix = _dot(hi, later) + _dot(lo, later)
        a = jnp.exp(log_beta + suffix + tail)
        if masked:
            a = jnp.where(strict, a, 0.0)
        acc = acc + _dot(a.astype(BF16), v)
        tail = tail + suffix[:, 0:1] + log_not[:, 0:1]
        return tail, acc

    def q_block(i, carry):
        rows = pl.ds(pl.multiple_of(i * t, t), t)
        q2 = _stack_halves(q_ref[0, rows, :])
        init = (jnp.zeros((2 * t, 1), F32), jnp.zeros((2 * t, HEAD_LANES), F32))
        c = update(q2, i, True, init)
        _, acc = lax.fori_loop(0, i, lambda n, cc: update(q2, i - 1 - n, False, cc), c)
        o_ref[0, rows, :] = jnp.where(lane < HALF, acc[:t], acc[t:]).astype(BF16)
        return carry

    lax.fori_loop(0, n_blocks, q_block, 0)


def _stick_breaking_attention(qkv):
    b, s, d3 = qkv.shape
    d = d3 // 3
    nh, spec = _attn_specs(d)
    return pl.pallas_call(
        _sb_kernel,
        grid=(b, nh),
        in_specs=[spec(0, s), spec(1, s), spec(2, s)],
        out_specs=pl.BlockSpec((1, s, HEAD_LANES), lambda bi, h: (bi, 0, h)),
        out_shape=jax.ShapeDtypeStruct((b, s, d), BF16),
        compiler_params=_params(2),
        name="stick_breaking_attention",
    )(qkv, qkv, qkv)


def _oproj_kernel(x_ref, a_ref, g_ref, w_ref, o_ref):
    o_ref[...] = x_ref[...] + _rms(_dot(a_ref[...], w_ref[...]), g_ref[...])


def _out_proj(x, attn, g, w_o):
    n, d = x.shape
    row = pl.BlockSpec((ROW_TILE, d), lambda i: (i, 0))
    return pl.pallas_call(
        _oproj_kernel,
        grid=(n // ROW_TILE,),
        in_specs=[row, row, _resident((1, d)), _resident((d, d))],
        out_specs=row,
        out_shape=jax.ShapeDtypeStruct((n, d), F32),
        compiler_params=_params(1),
        name="out_proj",
    )(x, attn, g.reshape(1, d), w_o.astype(BF16))


def _ple_kernel(x_ref, p_ref, gpre_ref, gpost_ref, wproj_ref, wgate_ref, o_ref):
    x = x_ref[...]
    e = _dot(p_ref[...].astype(BF16), wproj_ref[...])
    gate = jax.nn.sigmoid(_dot(_rms(x, gpre_ref[...]).astype(BF16), wgate_ref[...]))
    o_ref[...] = x + _rms(gate * e, gpost_ref[...])


def _ple(x, p, g_pre, g_post, w_proj, w_gate):
    n, d = x.shape
    dp = p.shape[1]
    row = pl.BlockSpec((ROW_TILE, d), lambda i: (i, 0))
    return pl.pallas_call(
        _ple_kernel,
        grid=(n // ROW_TILE,),
        in_specs=[row, pl.BlockSpec((ROW_TILE, dp), lambda i: (i, 0)),
                  _resident((1, d)), _resident((1, d)),
                  _resident((dp, d)), _resident((d, d))],
        out_specs=row,
        out_shape=jax.ShapeDtypeStruct((n, d), F32),
        compiler_params=_params(1),
        name="ple",
    )(x, p, g_pre.reshape(1, d), g_post.reshape(1, d),
      w_proj.astype(BF16), w_gate.astype(BF16))


def kernel(x, p, norm_gains, ffn1_w_in, ffn1_w_out, ffn2_w_in, ffn2_w_out, da_w_qkv, da_w_o,
           da_lambda, da_subln, sb_w_qkv, sb_w_o, ple_w_proj, ple_w_gate):
    b, s, d = x.shape
    depth = p.shape[0]
    n = b * s
    xf = x.reshape(n, d)
    for i in range(depth):
        g = norm_gains[i]
        j = i // 2
        xf = _ffn(xf, g[0], g[1], ffn1_w_in[i], ffn1_w_out[i])
        if i % 2 == 0:
            qkv = _qkv_proj(xf, g[2], da_w_qkv[j]).reshape(b, s, 3 * d)
            lambda_init = 0.8 - 0.6 * math.exp(-0.3 * i)
            attn = _diff_attention(qkv, da_lambda[j], da_subln[j], lambda_init)
            w_o = da_w_o[j]
        else:
            qkv = _qkv_proj(xf, g[2], sb_w_qkv[j]).reshape(b, s, 3 * d)
            attn = _stick_breaking_attention(qkv)
            w_o = sb_w_o[j]
        xf = _out_proj(xf, attn.reshape(n, d), g[3], w_o)
        xf = _ffn(xf, g[4], g[5], ffn2_w_in[i], ffn2_w_out[i])
        xf = _ple(xf, p[i].reshape(n, p.shape[-1]), g[6], g[7], ple_w_proj[i], ple_w_gate[i])
    return xf.reshape(b, s, d)
```

```python
import functools
import math

import jax
import jax.numpy as jnp
from jax import lax
from jax.experimental import pallas as pl
from jax.experimental.pallas import tpu as pltpu

EPS = 1e-6
NEG_BIG = -1e30
LOG2E = math.log2(math.e)
CHUNK = 64
HEAD_LANES = 128
HALF = 64
QK_SCALE = 1.0 / math.sqrt(HALF)
ATT_BLOCK = 256
DA_STREAMS = 2
SB_STREAMS = 4
SB_UNIT_ROWS = 256
ROW_TILE = 512
FF_CHUNK = 256
QKV_CHUNK = 512
VMEM_LIMIT = 56 * 1024 * 1024

F32 = jnp.float32
BF16 = jnp.bfloat16


def _rms(x, g):
    ms = jnp.mean(x * x, axis=-1, keepdims=True)
    return x * lax.rsqrt(ms + EPS) * g


def _dot(a, b):
    return lax.dot_general(a, b, (((1,), (0,)), ((), ())), preferred_element_type=F32)


def _dot_nt(a, b):
    return lax.dot_general(a, b, (((1,), (1,)), ((), ())), preferred_element_type=F32)


def _resident(shape):
    zeros = (0,) * len(shape)
    return pl.BlockSpec(shape, lambda *_: zeros, pipeline_mode=pl.Buffered(1))


def _params(n_axes):
    return pltpu.CompilerParams(
        dimension_semantics=("arbitrary",) * n_axes, vmem_limit_bytes=VMEM_LIMIT)


def _ffn_kernel(x_ref, gpre_ref, gpost_ref, win_ref, wout_ref, o_ref, h_ref, acc_ref):
    x = x_ref[...]
    h_ref[...] = _rms(x, gpre_ref[...]).astype(BF16)
    acc_ref[...] = jnp.zeros_like(acc_ref)

    def body(j, carry):
        h = h_ref[...]
        a = _dot(h, win_ref[0, j])
        b = _dot(h, win_ref[1, j])
        g = (a * jax.nn.sigmoid(a) * b).astype(BF16)
        acc_ref[...] += _dot(g, wout_ref[j])
        return carry

    lax.fori_loop(0, win_ref.shape[1], body, 0)
    o_ref[...] = x + 0.5 * _rms(acc_ref[...], gpost_ref[...])


def _ffn(x, g_pre, g_post, w_in, w_out):
    n, d = x.shape
    f = w_out.shape[0]
    nf = f // FF_CHUNK
    win = w_in.reshape(d, 2, nf, FF_CHUNK).transpose(1, 2, 0, 3).astype(BF16)
    wout = w_out.reshape(nf, FF_CHUNK, d).astype(BF16)
    row = pl.BlockSpec((ROW_TILE, d), lambda i: (i, 0))
    return pl.pallas_call(
        _ffn_kernel,
        grid=(n // ROW_TILE,),
        in_specs=[row, _resident((1, d)), _resident((1, d)),
                  _resident(win.shape), _resident(wout.shape)],
        out_specs=row,
        out_shape=jax.ShapeDtypeStruct((n, d), F32),
        scratch_shapes=[pltpu.VMEM((ROW_TILE, d), BF16), pltpu.VMEM((ROW_TILE, d), F32)],
        compiler_params=_params(1),
        name="ffn",
    )(x, g_pre.reshape(1, d), g_post.reshape(1, d), win, wout)


def _qkv_kernel(x_ref, g_ref, w_ref, o_ref):
    h = _rms(x_ref[...], g_ref[...]).astype(BF16)
    for c in range(w_ref.shape[1] // QKV_CHUNK):
        cols = slice(c * QKV_CHUNK, (c + 1) * QKV_CHUNK)
        o_ref[:, cols] = _dot(h, w_ref[:, cols]).astype(BF16)


def _qkv_proj(x, g, w_qkv):
    n, d = x.shape
    d3 = w_qkv.shape[1]
    return pl.pallas_call(
        _qkv_kernel,
        grid=(n // ROW_TILE,),
        in_specs=[pl.BlockSpec((ROW_TILE, d), lambda i: (i, 0)),
                  _resident((1, d)), _resident((d, d3))],
        out_specs=pl.BlockSpec((ROW_TILE, d3), lambda i: (i, 0)),
        out_shape=jax.ShapeDtypeStruct((n, d3), BF16),
        compiler_params=_params(1),
        name="qkv_proj",
    )(x, g.reshape(1, d), w_qkv.astype(BF16))


def _stack_halves(q):
    lane = lax.broadcasted_iota(jnp.int32, q.shape, 1)
    qs = q.astype(F32) * QK_SCALE
    lo = jnp.where(lane < HALF, qs, 0.0)
    hi = jnp.where(lane >= HALF, qs, 0.0)
    return jnp.concatenate([lo, hi], axis=0).astype(BF16)


def _attn_specs(d, seq, n_streams):
    width = n_streams * HEAD_LANES
    n_groups = d // width

    def spec(third):
        return pl.BlockSpec((1, seq, width), lambda b, h, *_: (b, 0, third * n_groups + h))
    return n_groups, spec


def _lanes(s):
    return slice(s * HEAD_LANES, (s + 1) * HEAD_LANES)


def _da_kernel(slope_ref, lamv_ref, subln_ref, q_ref, k_ref, v_ref, o_ref, *, lambda_init):
    t = ATT_BLOCK
    n_blocks = q_ref.shape[1] // t
    streams = range(DA_STREAMS)
    slopes = [slope_ref[pl.program_id(1) * DA_STREAMS + s] for s in streams]

    lv = lamv_ref[...]
    lam = (jnp.exp(jnp.sum(lv[0:1] * lv[1:2], axis=1, keepdims=True))
           - jnp.exp(jnp.sum(lv[2:3] * lv[3:4], axis=1, keepdims=True)) + lambda_init)
    subln = subln_ref[...]

    row = lax.broadcasted_iota(jnp.int32, (2 * t, t), 0) % t
    col = lax.broadcasted_iota(jnp.int32, (2 * t, t), 1)
    diag_shape = (row - jnp.abs(row - col)).astype(F32)
    allowed = col // CHUNK <= row // CHUNK
    diag_bias = [jnp.where(allowed, slopes[s] * diag_shape, NEG_BIG) for s in streams]
    col_f = lax.broadcasted_iota(jnp.int32, (1, t), 1).astype(F32)

    def update(s, q2, j, bias, carry):
        m, l, acc = carry
        keys = pl.ds(pl.multiple_of(j * t, t), t)
        sc = _dot_nt(q2, k_ref[0, keys, _lanes(s)]) + bias
        m_new = jnp.maximum(m, jnp.max(sc, axis=1, keepdims=True))
        alpha = jnp.exp(m - m_new)
        p = jnp.exp(sc - m_new)
        l = alpha * l + jnp.sum(p, axis=1, keepdims=True)
        acc = alpha * acc + _dot(p.astype(BF16), v_ref[0, keys, _lanes(s)])
        return m_new, l, acc

    def q_block(i, carry):
        rows = pl.ds(pl.multiple_of(i * t, t), t)
        q2 = [_stack_halves(q_ref[0, rows, _lanes(s)]) for s in streams]

        def k_block(j, cs):
            back = ((i - j) * t).astype(F32)
            return tuple(update(s, q2[s], j, slopes[s] * (col_f - back), cs[s]) for s in streams)

        init = (jnp.full((2 * t, 1), NEG_BIG, F32), jnp.zeros((2 * t, 1), F32),
                jnp.zeros((2 * t, HEAD_LANES), F32))
        cs = lax.fori_loop(0, i, k_block, (init,) * DA_STREAMS)
        for s in streams:
            _, l, acc = update(s, q2[s], i, diag_bias[s], cs[s])
            o = acc[:t] / l[:t] - lam * (acc[t:] / l[t:])
            o = _rms(o, subln) * (1.0 - lambda_init)
            o_ref[0, rows, _lanes(s)] = o.astype(BF16)
        return carry

    lax.fori_loop(0, n_blocks, q_block, 0)


def _diff_attention(qkv, lam_vecs, subln_g, lambda_init):
    b, s, d3 = qkv.shape
    d = d3 // 3
    nh = d // HEAD_LANES
    n_groups, spec = _attn_specs(d, s, DA_STREAMS)
    slopes = jnp.asarray([2.0 ** (-8.0 * (h + 1) / nh) for h in range(nh)], F32)
    grid_spec = pltpu.PrefetchScalarGridSpec(
        num_scalar_prefetch=1,
        grid=(b, n_groups),
        in_specs=[pl.BlockSpec(lam_vecs.shape, lambda *_: (0, 0)),
                  pl.BlockSpec((1, HEAD_LANES), lambda *_: (0, 0)),
                  spec(0), spec(1), spec(2)],
        out_specs=spec(0),
    )
    return pl.pallas_call(
        functools.partial(_da_kernel, lambda_init=lambda_init),
        grid_spec=grid_spec,
        out_shape=jax.ShapeDtypeStruct((b, s, d), BF16),
        compiler_params=_params(2),
        name="diff_attention",
    )(slopes, lam_vecs, subln_g.reshape(1, HEAD_LANES), qkv, qkv, qkv)


def _sb_kernel(q_ref, k_ref, v_ref, o_ref, q2_ref, tail_ref, acc_ref):
    t = ATT_BLOCK
    n_blocks = q_ref.shape[1] // t
    streams = range(SB_STREAMS)
    cr = SB_UNIT_ROWS
    per_stream = 2 * t // cr
    units = [(s, c) for s in streams for c in range(per_stream)]

    row = lax.broadcasted_iota(jnp.int32, (2 * t, t), 0) % t
    col = lax.broadcasted_iota(jnp.int32, (2 * t, t), 1)
    strict = col < row
    strict_u = [strict[c * cr:(c + 1) * cr] for c in range(per_stream)]
    kr = lax.broadcasted_iota(jnp.int32, (t, t), 0)
    kc = lax.broadcasted_iota(jnp.int32, (t, t), 1)
    later = (kr > kc).astype(BF16)
    lane = lax.broadcasted_iota(jnp.int32, (t, HEAD_LANES), 1)

    def update(j, first):
        keys = pl.ds(pl.multiple_of(j * t, t), t)
        ws, mids = {}, {}

        def stage_scores(u):
            s, c = units[u]
            ws[u] = _dot_nt(q2_ref[s, c * cr:(c + 1) * cr, :], k_ref[0, keys, _lanes(s)]) * LOG2E

        def stage_suffix(u):
            w = ws.pop(u)
            log_beta = jnp.minimum(w, 0.0) - jnp.log2(1.0 + jnp.exp2(-jnp.abs(w)))
            log_not = log_beta - w
            if first:
                log_not = jnp.where(strict_u[units[u][1]], log_not, 0.0)
            hi = log_not.astype(BF16)
            lo = (log_not - hi.astype(F32)).astype(BF16)
            suffix = _dot(hi, later) + _dot(lo, later)
            total = jnp.broadcast_to(suffix[:, 0:1] + log_not[:, 0:1], (cr, HEAD_LANES))
            mids[u] = (log_beta + suffix, total)

        def stage_values(u):
            s, c = units[u]
            pre, total = mids.pop(u)
            if first:
                a = jnp.where(strict_u[c], jnp.exp2(pre), 0.0)
                acc_ref[u] = _dot(a.astype(BF16), v_ref[0, keys, _lanes(s)])
                tail_ref[u] = total
            else:
                tail = tail_ref[u]
                a = jnp.exp2(pre + jnp.concatenate([tail] * (t // HEAD_LANES), axis=1))
                acc_ref[u] += _dot(a.astype(BF16), v_ref[0, keys, _lanes(s)])
                tail_ref[u] = tail + total

        n = len(units)
        for step in range(n + 2):
            if step < n:
                stage_scores(step)
            if 0 <= step - 1 < n:
                stage_suffix(step - 1)
            if 0 <= step - 2 < n:
                stage_values(step - 2)

    def q_block(i, carry):
        rows = pl.ds(pl.multiple_of(i * t, t), t)
        for s in streams:
            q2_ref[s] = _stack_halves(q_ref[0, rows, _lanes(s)])
        update(i, True)

        def k_block(n, c):
            update(i - 1 - n, False)
            return c

        lax.fori_loop(0, i, k_block, 0)
        for s in streams:
            acc = jnp.concatenate([acc_ref[s * per_stream + c] for c in range(per_stream)], axis=0)
            o_ref[0, rows, _lanes(s)] = jnp.where(lane < HALF, acc[:t], acc[t:]).astype(BF16)
        return carry

    lax.fori_loop(0, n_blocks, q_block, 0)


def _stick_breaking_attention(qkv):
    b, s, d3 = qkv.shape
    d = d3 // 3
    n_groups, spec = _attn_specs(d, s, SB_STREAMS)
    n_units = SB_STREAMS * 2 * ATT_BLOCK // SB_UNIT_ROWS
    return pl.pallas_call(
        _sb_kernel,
        grid=(b, n_groups),
        in_specs=[spec(0), spec(1), spec(2)],
        out_specs=spec(0),
        out_shape=jax.ShapeDtypeStruct((b, s, d), BF16),
        scratch_shapes=[pltpu.VMEM((SB_STREAMS, 2 * ATT_BLOCK, HEAD_LANES), BF16),
                        pltpu.VMEM((n_units, SB_UNIT_ROWS, HEAD_LANES), F32),
                        pltpu.VMEM((n_units, SB_UNIT_ROWS, HEAD_LANES), F32)],
        compiler_params=_params(2),
        name="stick_breaking_attention",
    )(qkv, qkv, qkv)


def _oproj_kernel(x_ref, a_ref, g_ref, w_ref, o_ref):
    o_ref[...] = x_ref[...] + _rms(_dot(a_ref[...], w_ref[...]), g_ref[...])


def _out_proj(x, attn, g, w_o):
    n, d = x.shape
    row = pl.BlockSpec((ROW_TILE, d), lambda i: (i, 0))
    return pl.pallas_call(
        _oproj_kernel,
        grid=(n // ROW_TILE,),
        in_specs=[row, row, _resident((1, d)), _resident((d, d))],
        out_specs=row,
        out_shape=jax.ShapeDtypeStruct((n, d), F32),
        compiler_params=_params(1),
        name="out_proj",
    )(x, attn, g.reshape(1, d), w_o.astype(BF16))


def _ple_kernel(x_ref, p_ref, gpre_ref, gpost_ref, wproj_ref, wgate_ref, o_ref):
    x = x_ref[...]
    e = _dot(p_ref[...].astype(BF16), wproj_ref[...])
    gate = jax.nn.sigmoid(_dot(_rms(x, gpre_ref[...]).astype(BF16), wgate_ref[...]))
    o_ref[...] = x + _rms(gate * e, gpost_ref[...])


def _ple(x, p, g_pre, g_post, w_proj, w_gate):
    n, d = x.shape
    dp = p.shape[1]
    row = pl.BlockSpec((ROW_TILE, d), lambda i: (i, 0))
    return pl.pallas_call(
        _ple_kernel,
        grid=(n // ROW_TILE,),
        in_specs=[row, pl.BlockSpec((ROW_TILE, dp), lambda i: (i, 0)),
                  _resident((1, d)), _resident((1, d)),
                  _resident((dp, d)), _resident((d, d))],
        out_specs=row,
        out_shape=jax.ShapeDtypeStruct((n, d), F32),
        compiler_params=_params(1),
        name="ple",
    )(x, p, g_pre.reshape(1, d), g_post.reshape(1, d),
      w_proj.astype(BF16), w_gate.astype(BF16))


def kernel(x, p, norm_gains, ffn1_w_in, ffn1_w_out, ffn2_w_in, ffn2_w_out, da_w_qkv, da_w_o,
           da_lambda, da_subln, sb_w_qkv, sb_w_o, ple_w_proj, ple_w_gate):
    b, s, d = x.shape
    depth = p.shape[0]
    n = b * s
    xf = x.reshape(n, d)
    for i in range(depth):
        g = norm_gains[i]
        j = i // 2
        xf = _ffn(xf, g[0], g[1], ffn1_w_in[i], ffn1_w_out[i])
        if i % 2 == 0:
            qkv = _qkv_proj(xf, g[2], da_w_qkv[j]).reshape(b, s, 3 * d)
            lambda_init = 0.8 - 0.6 * math.exp(-0.3 * i)
            attn = _diff_attention(qkv, da_lambda[j], da_subln[j], lambda_init)
            w_o = da_w_o[j]
        else:
            qkv = _qkv_proj(xf, g[2], sb_w_qkv[j]).reshape(b, s, 3 * d)
            attn = _stick_breaking_attention(qkv)
            w_o = sb_w_o[j]
        xf = _out_proj(xf, attn.reshape(n, d), g[3], w_o)
        xf = _ffn(xf, g[4], g[5], ffn2_w_in[i], ffn2_w_out[i])
        xf = _ple(xf, p[i].reshape(n, p.shape[-1]), g[6], g[7], ple_w_proj[i], ple_w_gate[i])
    return xf.reshape(b, s, d)
```

```python
import functools
import math

import jax
import jax.numpy as jnp
from jax import lax
from jax.experimental import pallas as pl
from jax.experimental.pallas import tpu as pltpu

EPS = 1e-6
NEG_BIG = -1e30
LOG2E = math.log2(math.e)
CHUNK = 64
HEAD_LANES = 128
HALF = 64
QK_SCALE = 1.0 / math.sqrt(HALF)
ATT_BLOCK = 256
DA_STREAMS = 4
SB_STREAMS = 4
SB_UNIT_ROWS = 256
ROW_TILE = 512
FFN_ROW_TILE = 512
FF_CHUNK = 256
QKV_CHUNK = 512
VMEM_LIMIT = 56 * 1024 * 1024

F32 = jnp.float32
BF16 = jnp.bfloat16


def _rms(x, g):
    ms = jnp.mean(x * x, axis=-1, keepdims=True)
    return x * lax.rsqrt(ms + EPS) * g


def _dot(a, b):
    return lax.dot_general(a, b, (((1,), (0,)), ((), ())), preferred_element_type=F32)


def _dot_nt(a, b):
    return lax.dot_general(a, b, (((1,), (1,)), ((), ())), preferred_element_type=F32)


def _resident(shape):
    zeros = (0,) * len(shape)
    return pl.BlockSpec(shape, lambda *_: zeros, pipeline_mode=pl.Buffered(1))


def _params(n_axes):
    return pltpu.CompilerParams(
        dimension_semantics=("arbitrary",) * n_axes, vmem_limit_bytes=VMEM_LIMIT)


def _ffn_kernel(x_ref, gpre_ref, gpost_ref, win_ref, wout_ref, o_ref, h_ref, acc_ref):
    x = x_ref[...]
    h_ref[...] = _rms(x, gpre_ref[...]).astype(BF16)
    acc_ref[...] = jnp.zeros_like(acc_ref)

    for j in range(win_ref.shape[1]):
        h = h_ref[...]
        a = _dot(h, win_ref[0, j])
        b = _dot(h, win_ref[1, j])
        g = (a * jax.nn.sigmoid(a) * b).astype(BF16)
        acc_ref[...] += _dot(g, wout_ref[j])
    o_ref[...] = x + 0.5 * _rms(acc_ref[...], gpost_ref[...])


def _ffn(x, g_pre, g_post, w_in, w_out):
    n, d = x.shape
    f = w_out.shape[0]
    nf = f // FF_CHUNK
    win = w_in.reshape(d, 2, nf, FF_CHUNK).transpose(1, 2, 0, 3).astype(BF16)
    wout = w_out.reshape(nf, FF_CHUNK, d).astype(BF16)
    row = pl.BlockSpec((FFN_ROW_TILE, d), lambda i: (i, 0))
    return pl.pallas_call(
        _ffn_kernel,
        grid=(n // FFN_ROW_TILE,),
        in_specs=[row, _resident((1, d)), _resident((1, d)),
                  _resident(win.shape), _resident(wout.shape)],
        out_specs=row,
        out_shape=jax.ShapeDtypeStruct((n, d), F32),
        scratch_shapes=[pltpu.VMEM((FFN_ROW_TILE, d), BF16), pltpu.VMEM((FFN_ROW_TILE, d), F32)],
        compiler_params=_params(1),
        name="ffn",
    )(x, g_pre.reshape(1, d), g_post.reshape(1, d), win, wout)


def _qkv_kernel(x_ref, g_ref, w_ref, o_ref):
    h = _rms(x_ref[...], g_ref[...]).astype(BF16)
    for c in range(w_ref.shape[1] // QKV_CHUNK):
        cols = slice(c * QKV_CHUNK, (c + 1) * QKV_CHUNK)
        o_ref[:, cols] = _dot(h, w_ref[:, cols]).astype(BF16)


def _qkv_proj(x, g, w_qkv):
    n, d = x.shape
    d3 = w_qkv.shape[1]
    return pl.pallas_call(
        _qkv_kernel,
        grid=(n // ROW_TILE,),
        in_specs=[pl.BlockSpec((ROW_TILE, d), lambda i: (i, 0)),
                  _resident((1, d)), _resident((d, d3))],
        out_specs=pl.BlockSpec((ROW_TILE, d3), lambda i: (i, 0)),
        out_shape=jax.ShapeDtypeStruct((n, d3), BF16),
        compiler_params=_params(1),
        name="qkv_proj",
    )(x, g.reshape(1, d), w_qkv.astype(BF16))


def _stack_halves(q):
    lane = lax.broadcasted_iota(jnp.int32, q.shape, 1)
    qs = q.astype(F32) * QK_SCALE
    lo = jnp.where(lane < HALF, qs, 0.0)
    hi = jnp.where(lane >= HALF, qs, 0.0)
    return jnp.concatenate([lo, hi], axis=0).astype(BF16)


def _attn_specs(d, seq, n_streams):
    width = n_streams * HEAD_LANES
    n_groups = d // width

    def spec(third):
        return pl.BlockSpec((1, seq, width), lambda b, h, *_: (b, 0, third * n_groups + h))
    return n_groups, spec


def _lanes(s):
    return slice(s * HEAD_LANES, (s + 1) * HEAD_LANES)


def _da_kernel(slope_ref, lamv_ref, subln_ref, q_ref, k_ref, v_ref, o_ref,
               q2_ref, s_ref, mx_ref, l_ref, acc_ref, *, lambda_init):
    t = ATT_BLOCK
    n_blocks = q_ref.shape[1] // t
    streams = range(DA_STREAMS)
    slopes = [slope_ref[pl.program_id(1) * DA_STREAMS + s] for s in streams]

    lv = lamv_ref[...]
    lam = (jnp.exp(jnp.sum(lv[0:1] * lv[1:2], axis=1, keepdims=True))
           - jnp.exp(jnp.sum(lv[2:3] * lv[3:4], axis=1, keepdims=True)) + lambda_init)
    subln = subln_ref[...]

    row = lax.broadcasted_iota(jnp.int32, (t, t), 0)
    col = lax.broadcasted_iota(jnp.int32, (t, t), 1)
    diag_shape = (row - jnp.abs(row - col)).astype(F32)
    allowed = col // CHUNK <= row // CHUNK
    diag_bias = [jnp.where(allowed, slopes[s] * diag_shape, NEG_BIG) for s in streams]
    col_f = lax.broadcasted_iota(jnp.int32, (1, t), 1).astype(F32)

    units = [(s, c) for s in streams for c in range(2)]
    n_lane_tiles = t // HEAD_LANES

    def lane_fold(x, op):
        out = x[:, :HEAD_LANES]
        for c in range(1, n_lane_tiles):
            out = op(out, x[:, c * HEAD_LANES:(c + 1) * HEAD_LANES])
        return out

    def score_pass(j, bias, first):
        keys = pl.ds(pl.multiple_of(j * t, t), t)
        raw = {}

        def matmul(u):
            s, c = units[u]
            raw[u] = _dot_nt(q2_ref[s, c * t:(c + 1) * t, :], k_ref[0, keys, _lanes(s)])

        def finish(u):
            sc = raw.pop(u) + bias[units[u][0]]
            s_ref[u, j] = sc
            tile_max = lane_fold(sc, jnp.maximum)
            mx_ref[u] = tile_max if first else jnp.maximum(mx_ref[u], tile_max)

        for step in range(len(units) + 1):
            if step < len(units):
                matmul(step)
            if step >= 1:
                finish(step - 1)

    def value_pass(j, first):
        keys = pl.ds(pl.multiple_of(j * t, t), t)
        for u, (s, c) in enumerate(units):
            m = mx_ref[u]
            p = jnp.exp(s_ref[u, j] - jnp.concatenate([m] * n_lane_tiles, axis=1))
            pv = _dot(p.astype(BF16), v_ref[0, keys, _lanes(s)])
            psum = lane_fold(p, jnp.add)
            if first:
                l_ref[u] = psum
                acc_ref[u] = pv
            else:
                l_ref[u] += psum
                acc_ref[u] += pv

    def q_block(i, carry):
        rows = pl.ds(pl.multiple_of(i * t, t), t)
        for s in streams:
            q2_ref[s] = _stack_halves(q_ref[0, rows, _lanes(s)])

        score_pass(i, diag_bias, True)

        def earlier_scores(j, c):
            back = jnp.asarray((i - j) * t, F32)
            score_pass(j, [slopes[s] * (col_f - back) for s in streams], False)
            return c

        lax.fori_loop(0, i, earlier_scores, 0)
        for u in range(len(units)):
            row_max = jnp.max(mx_ref[u], axis=1, keepdims=True)
            mx_ref[u] = jnp.broadcast_to(row_max, (t, HEAD_LANES))

        value_pass(0, True)

        def later_values(j, c):
            value_pass(j, False)
            return c

        lax.fori_loop(1, i + 1, later_values, 0)
        for s in streams:
            l0 = jnp.sum(l_ref[2 * s], axis=1, keepdims=True)
            l1 = jnp.sum(l_ref[2 * s + 1], axis=1, keepdims=True)
            o = acc_ref[2 * s] / l0 - lam * (acc_ref[2 * s + 1] / l1)
            o = _rms(o, subln) * (1.0 - lambda_init)
            o_ref[0, rows, _lanes(s)] = o.astype(BF16)
        return carry

    lax.fori_loop(0, n_blocks, q_block, 0)


def _diff_attention(qkv, lam_vecs, subln_g, lambda_init):
    b, s, d3 = qkv.shape
    d = d3 // 3
    nh = d // HEAD_LANES
    n_groups, spec = _attn_specs(d, s, DA_STREAMS)
    slopes = jnp.asarray([2.0 ** (-8.0 * (h + 1) / nh) for h in range(nh)], F32)
    grid_spec = pltpu.PrefetchScalarGridSpec(
        num_scalar_prefetch=1,
        grid=(b, n_groups),
        in_specs=[pl.BlockSpec(lam_vecs.shape, lambda *_: (0, 0)),
                  pl.BlockSpec((1, HEAD_LANES), lambda *_: (0, 0)),
                  spec(0), spec(1), spec(2)],
        out_specs=spec(0),
        scratch_shapes=[
            pltpu.VMEM((DA_STREAMS, 2 * ATT_BLOCK, HEAD_LANES), BF16),
            pltpu.VMEM((2 * DA_STREAMS, s // ATT_BLOCK, ATT_BLOCK, ATT_BLOCK), F32),
            pltpu.VMEM((2 * DA_STREAMS, ATT_BLOCK, HEAD_LANES), F32),
            pltpu.VMEM((2 * DA_STREAMS, ATT_BLOCK, HEAD_LANES), F32),
            pltpu.VMEM((2 * DA_STREAMS, ATT_BLOCK, HEAD_LANES), F32)],
    )
    return pl.pallas_call(
        functools.partial(_da_kernel, lambda_init=lambda_init),
        grid_spec=grid_spec,
        out_shape=jax.ShapeDtypeStruct((b, s, d), BF16),
        compiler_params=_params(2),
        name="diff_attention",
    )(slopes, lam_vecs, subln_g.reshape(1, HEAD_LANES), qkv, qkv, qkv)


def _sb_kernel(q_ref, k_ref, v_ref, o_ref, q2_ref, tail_ref, acc_ref):
    t = ATT_BLOCK
    n_blocks = q_ref.shape[1] // t
    streams = range(SB_STREAMS)
    cr = SB_UNIT_ROWS
    per_stream = 2 * t // cr
    units = [(s, c) for s in streams for c in range(per_stream)]

    row = lax.broadcasted_iota(jnp.int32, (2 * t, t), 0) % t
    col = lax.broadcasted_iota(jnp.int32, (2 * t, t), 1)
    strict = col < row
    strict_u = [strict[c * cr:(c + 1) * cr] for c in range(per_stream)]
    kr = lax.broadcasted_iota(jnp.int32, (t, t), 0)
    kc = lax.broadcasted_iota(jnp.int32, (t, t), 1)
    later = (kr > kc).astype(BF16)
    lane = lax.broadcasted_iota(jnp.int32, (t, HEAD_LANES), 1)

    def update(j, first):
        keys = pl.ds(pl.multiple_of(j * t, t), t)
        ws, mids = {}, {}

        def stage_scores(u):
            s, c = units[u]
            ws[u] = _dot_nt(q2_ref[s, c * cr:(c + 1) * cr, :], k_ref[0, keys, _lanes(s)]) * LOG2E

        def stage_suffix(u):
            w = ws.pop(u)
            log_beta = jnp.minimum(w, 0.0) - jnp.log2(1.0 + jnp.exp2(-jnp.abs(w)))
            log_not = log_beta - w
            if first:
                log_not = jnp.where(strict_u[units[u][1]], log_not, 0.0)
            hi = log_not.astype(BF16)
            lo = (log_not - hi.astype(F32)).astype(BF16)
            suffix = _dot(hi, later) + _dot(lo, later)
            total = jnp.broadcast_to(suffix[:, 0:1] + log_not[:, 0:1], (cr, HEAD_LANES))
            mids[u] = (log_beta + suffix, total)

        def stage_values(u):
            s, c = units[u]
            pre, total = mids.pop(u)
            if first:
                a = jnp.where(strict_u[c], jnp.exp2(pre), 0.0)
                acc_ref[u] = _dot(a.astype(BF16), v_ref[0, keys, _lanes(s)])
                tail_ref[u] = total
            else:
                tail = tail_ref[u]
                a = jnp.exp2(pre + jnp.concatenate([tail] * (t // HEAD_LANES), axis=1))
                acc_ref[u] += _dot(a.astype(BF16), v_ref[0, keys, _lanes(s)])
                tail_ref[u] = tail + total

        n = len(units)
        for step in range(n + 2):
            if step < n:
                stage_scores(step)
            if 0 <= step - 1 < n:
                stage_suffix(step - 1)
            if 0 <= step - 2 < n:
                stage_values(step - 2)

    def q_block(i, carry):
        rows = pl.ds(pl.multiple_of(i * t, t), t)
        for s in streams:
            q2_ref[s] = _stack_halves(q_ref[0, rows, _lanes(s)])
        update(i, True)

        def k_block(n, c):
            update(i - 1 - n, False)
            return c

        lax.fori_loop(0, i, k_block, 0)
        for s in streams:
            acc = jnp.concatenate([acc_ref[s * per_stream + c] for c in range(per_stream)], axis=0)
            o_ref[0, rows, _lanes(s)] = jnp.where(lane < HALF, acc[:t], acc[t:]).astype(BF16)
        return carry

    lax.fori_loop(0, n_blocks, q_block, 0)


def _stick_breaking_attention(qkv):
    b, s, d3 = qkv.shape
    d = d3 // 3
    n_groups, spec = _attn_specs(d, s, SB_STREAMS)
    n_units = SB_STREAMS * 2 * ATT_BLOCK // SB_UNIT_ROWS
    return pl.pallas_call(
        _sb_kernel,
        grid=(b, n_groups),
        in_specs=[spec(0), spec(1), spec(2)],
        out_specs=spec(0),
        out_shape=jax.ShapeDtypeStruct((b, s, d), BF16),
        scratch_shapes=[pltpu.VMEM((SB_STREAMS, 2 * ATT_BLOCK, HEAD_LANES), BF16),
                        pltpu.VMEM((n_units, SB_UNIT_ROWS, HEAD_LANES), F32),
                        pltpu.VMEM((n_units, SB_UNIT_ROWS, HEAD_LANES), F32)],
        compiler_params=_params(2),
        name="stick_breaking_attention",
    )(qkv, qkv, qkv)


def _oproj_kernel(x_ref, a_ref, g_ref, w_ref, o_ref):
    o_ref[...] = x_ref[...] + _rms(_dot(a_ref[...], w_ref[...]), g_ref[...])


def _out_proj(x, attn, g, w_o):
    n, d = x.shape
    row = pl.BlockSpec((ROW_TILE, d), lambda i: (i, 0))
    return pl.pallas_call(
        _oproj_kernel,
        grid=(n // ROW_TILE,),
        in_specs=[row, row, _resident((1, d)), _resident((d, d))],
        out_specs=row,
        out_shape=jax.ShapeDtypeStruct((n, d), F32),
        compiler_params=_params(1),
        name="out_proj",
    )(x, attn, g.reshape(1, d), w_o.astype(BF16))


def _ple_kernel(x_ref, p_ref, gpre_ref, gpost_ref, wproj_ref, wgate_ref, o_ref):
    x = x_ref[...]
    e = _dot(p_ref[...].astype(BF16), wproj_ref[...])
    gate = jax.nn.sigmoid(_dot(_rms(x, gpre_ref[...]).astype(BF16), wgate_ref[...]))
    o_ref[...] = x + _rms(gate * e, gpost_ref[...])


def _ple(x, p, g_pre, g_post, w_proj, w_gate):
    n, d = x.shape
    dp = p.shape[1]
    row = pl.BlockSpec((ROW_TILE, d), lambda i: (i, 0))
    return pl.pallas_call(
        _ple_kernel,
        grid=(n // ROW_TILE,),
        in_specs=[row, pl.BlockSpec((ROW_TILE, dp), lambda i: (i, 0)),
                  _resident((1, d)), _resident((1, d)),
                  _resident((dp, d)), _resident((d, d))],
        out_specs=row,
        out_shape=jax.ShapeDtypeStruct((n, d), F32),
        compiler_params=_params(1),
        name="ple",
    )(x, p, g_pre.reshape(1, d), g_post.reshape(1, d),
      w_proj.astype(BF16), w_gate.astype(BF16))


def kernel(x, p, norm_gains, ffn1_w_in, ffn1_w_out, ffn2_w_in, ffn2_w_out, da_w_qkv, da_w_o,
           da_lambda, da_subln, sb_w_qkv, sb_w_o, ple_w_proj, ple_w_gate):
    b, s, d = x.shape
    depth = p.shape[0]
    n = b * s
    xf = x.reshape(n, d)
    for i in range(depth):
        g = norm_gains[i]
        j = i // 2
        xf = _ffn(xf, g[0], g[1], ffn1_w_in[i], ffn1_w_out[i])
        if i % 2 == 0:
            qkv = _qkv_proj(xf, g[2], da_w_qkv[j]).reshape(b, s, 3 * d)
            lambda_init = 0.8 - 0.6 * math.exp(-0.3 * i)
            attn = _diff_attention(qkv, da_lambda[j], da_subln[j], lambda_init)
            w_o = da_w_o[j]
        else:
            qkv = _qkv_proj(xf, g[2], sb_w_qkv[j]).reshape(b, s, 3 * d)
            attn = _stick_breaking_attention(qkv)
            w_o = sb_w_o[j]
        xf = _out_proj(xf, attn.reshape(n, d), g[3], w_o)
        xf = _ffn(xf, g[4], g[5], ffn2_w_in[i], ffn2_w_out[i])
        xf = _ple(xf, p[i].reshape(n, p.shape[-1]), g[6], g[7], ple_w_proj[i], ple_w_gate[i])
    return xf.reshape(b, s, d)
```

```python
import functools
import math

import jax
import jax.numpy as jnp
from jax import lax
from jax.experimental import pallas as pl
from jax.experimental.pallas import tpu as pltpu

EPS = 1e-6
NEG_BIG = -1e30
LOG2E = math.log2(math.e)
CHUNK = 64
HEAD_LANES = 128
HALF = 64
QK_SCALE = 1.0 / math.sqrt(HALF)
ATT_BLOCK = 256
DA_STREAMS = 4
SB_STREAMS = 4
SB_UNIT_ROWS = 256
ROW_TILE = 512
FFN_ROW_TILE = 512
FF_CHUNK = 256
QKV_CHUNK = 512
VMEM_LIMIT = 56 * 1024 * 1024

F32 = jnp.float32
BF16 = jnp.bfloat16


def _rms(x, g):
    ms = jnp.mean(x * x, axis=-1, keepdims=True)
    return x * lax.rsqrt(ms + EPS) * g


def _dot(a, b):
    return lax.dot_general(a, b, (((1,), (0,)), ((), ())), preferred_element_type=F32)


def _dot_nt(a, b):
    return lax.dot_general(a, b, (((1,), (1,)), ((), ())), preferred_element_type=F32)


def _resident(shape):
    zeros = (0,) * len(shape)
    return pl.BlockSpec(shape, lambda *_: zeros, pipeline_mode=pl.Buffered(1))


def _params(n_axes):
    return pltpu.CompilerParams(
        dimension_semantics=("arbitrary",) * n_axes, vmem_limit_bytes=VMEM_LIMIT)


def _swiglu_half_step(x, gpre, gpost, win_ref, wout_ref, h_ref, acc_ref):
    h_ref[...] = _rms(x, gpre).astype(BF16)
    acc_ref[...] = jnp.zeros_like(acc_ref)

    f = wout_ref.shape[0]
    for j in range(f // FF_CHUNK):
        lo = j * FF_CHUNK
        h = h_ref[...]
        a = _dot(h, win_ref[:, lo:lo + FF_CHUNK])
        b = _dot(h, win_ref[:, f + lo:f + lo + FF_CHUNK])
        g = (a * jax.nn.sigmoid(a) * b).astype(BF16)
        acc_ref[...] += _dot(g, wout_ref[lo:lo + FF_CHUNK, :])
    return x + 0.5 * _rms(acc_ref[...], gpost)


def _ffn_kernel(x_ref, gpre_ref, gpost_ref, win_ref, wout_ref, o_ref, h_ref, acc_ref):
    o_ref[...] = _swiglu_half_step(x_ref[...], gpre_ref[...], gpost_ref[...],
                                   win_ref, wout_ref, h_ref, acc_ref)


def _ffn(x, g_pre, g_post, w_in, w_out):
    n, d = x.shape
    win = w_in.astype(BF16)
    wout = w_out.astype(BF16)
    row = pl.BlockSpec((FFN_ROW_TILE, d), lambda i: (i, 0))
    return pl.pallas_call(
        _ffn_kernel,
        grid=(n // FFN_ROW_TILE,),
        in_specs=[row, _resident((1, d)), _resident((1, d)),
                  _resident(win.shape), _resident(wout.shape)],
        out_specs=row,
        out_shape=jax.ShapeDtypeStruct((n, d), F32),
        scratch_shapes=[pltpu.VMEM((FFN_ROW_TILE, d), BF16), pltpu.VMEM((FFN_ROW_TILE, d), F32)],
        compiler_params=_params(1),
        name="ffn",
    )(x, g_pre.reshape(1, d), g_post.reshape(1, d), win, wout)


def _qkv_kernel(x_ref, g_ref, w_ref, o_ref):
    h = _rms(x_ref[...], g_ref[...]).astype(BF16)
    for c in range(w_ref.shape[1] // QKV_CHUNK):
        cols = slice(c * QKV_CHUNK, (c + 1) * QKV_CHUNK)
        o_ref[:, cols] = _dot(h, w_ref[:, cols]).astype(BF16)


def _qkv_proj(x, g, w_qkv):
    n, d = x.shape
    d3 = w_qkv.shape[1]
    return pl.pallas_call(
        _qkv_kernel,
        grid=(n // ROW_TILE,),
        in_specs=[pl.BlockSpec((ROW_TILE, d), lambda i: (i, 0)),
                  _resident((1, d)), _resident((d, d3))],
        out_specs=pl.BlockSpec((ROW_TILE, d3), lambda i: (i, 0)),
        out_shape=jax.ShapeDtypeStruct((n, d3), BF16),
        compiler_params=_params(1),
        name="qkv_proj",
    )(x, g.reshape(1, d), w_qkv.astype(BF16))


def _stack_halves(q):
    lane = lax.broadcasted_iota(jnp.int32, q.shape, 1)
    qs = q.astype(F32) * QK_SCALE
    lo = jnp.where(lane < HALF, qs, 0.0)
    hi = jnp.where(lane >= HALF, qs, 0.0)
    return jnp.concatenate([lo, hi], axis=0).astype(BF16)


def _attn_specs(d, seq, n_streams):
    width = n_streams * HEAD_LANES
    n_groups = d // width

    def spec(third):
        return pl.BlockSpec((1, seq, width), lambda b, h, *_: (b, 0, third * n_groups + h))
    return n_groups, spec


def _lanes(s):
    return slice(s * HEAD_LANES, (s + 1) * HEAD_LANES)


def _da_kernel(slope_ref, lamv_ref, subln_ref, q_ref, k_ref, v_ref, o_ref,
               q2_ref, s_ref, mx_ref, l_ref, acc_ref, *, lambda_init):
    t = ATT_BLOCK
    n_blocks = q_ref.shape[1] // t
    streams = range(DA_STREAMS)
    slopes = [slope_ref[pl.program_id(1) * DA_STREAMS + s] for s in streams]

    lv = lamv_ref[...]
    lam = (jnp.exp(jnp.sum(lv[0:1] * lv[1:2], axis=1, keepdims=True))
           - jnp.exp(jnp.sum(lv[2:3] * lv[3:4], axis=1, keepdims=True)) + lambda_init)
    subln = subln_ref[...]

    row = lax.broadcasted_iota(jnp.int32, (t, t), 0)
    col = lax.broadcasted_iota(jnp.int32, (t, t), 1)
    diag_shape = (row - jnp.abs(row - col)).astype(F32)
    allowed = col // CHUNK <= row // CHUNK
    diag_bias = [jnp.where(allowed, slopes[s] * diag_shape, NEG_BIG) for s in streams]
    col_f = lax.broadcasted_iota(jnp.int32, (1, t), 1).astype(F32)

    units = [(s, c) for s in streams for c in range(2)]
    n_lane_tiles = t // HEAD_LANES

    def lane_fold(x, op):
        out = x[:, :HEAD_LANES]
        for c in range(1, n_lane_tiles):
            out = op(out, x[:, c * HEAD_LANES:(c + 1) * HEAD_LANES])
        return out

    def score_pass(j, bias, first):
        keys = pl.ds(pl.multiple_of(j * t, t), t)
        raw = {}

        def matmul(u):
            s, c = units[u]
            raw[u] = _dot_nt(q2_ref[s, c * t:(c + 1) * t, :], k_ref[0, keys, _lanes(s)])

        def finish(u):
            sc = raw.pop(u) + bias[units[u][0]]
            s_ref[u, j] = sc
            tile_max = lane_fold(sc, jnp.maximum)
            mx_ref[u] = tile_max if first else jnp.maximum(mx_ref[u], tile_max)

        for step in range(len(units) + 1):
            if step < len(units):
                matmul(step)
            if step >= 1:
                finish(step - 1)

    def value_pass(j, first):
        keys = pl.ds(pl.multiple_of(j * t, t), t)
        for u, (s, c) in enumerate(units):
            m = mx_ref[u]
            p = jnp.exp(s_ref[u, j] - jnp.concatenate([m] * n_lane_tiles, axis=1))
            pv = _dot(p.astype(BF16), v_ref[0, keys, _lanes(s)])
            psum = lane_fold(p, jnp.add)
            if first:
                l_ref[u] = psum
                acc_ref[u] = pv
            else:
                l_ref[u] += psum
                acc_ref[u] += pv

    def q_block(i, carry):
        rows = pl.ds(pl.multiple_of(i * t, t), t)
        for s in streams:
            q2_ref[s] = _stack_halves(q_ref[0, rows, _lanes(s)])

        score_pass(i, diag_bias, True)

        def earlier_scores(j, c):
            back = jnp.asarray((i - j) * t, F32)
            score_pass(j, [slopes[s] * (col_f - back) for s in streams], False)
            return c

        lax.fori_loop(0, i, earlier_scores, 0)
        for u in range(len(units)):
            row_max = jnp.max(mx_ref[u], axis=1, keepdims=True)
            mx_ref[u] = jnp.broadcast_to(row_max, (t, HEAD_LANES))

        value_pass(0, True)

        def later_values(j, c):
            value_pass(j, False)
            return c

        lax.fori_loop(1, i + 1, later_values, 0)
        for s in streams:
            l0 = jnp.sum(l_ref[2 * s], axis=1, keepdims=True)
            l1 = jnp.sum(l_ref[2 * s + 1], axis=1, keepdims=True)
            o = acc_ref[2 * s] / l0 - lam * (acc_ref[2 * s + 1] / l1)
            o = _rms(o, subln) * (1.0 - lambda_init)
            o_ref[0, rows, _lanes(s)] = o.astype(BF16)
        return carry

    lax.fori_loop(0, n_blocks, q_block, 0)


def _diff_attention(qkv, lam_vecs, subln_g, lambda_init):
    b, s, d3 = qkv.shape
    d = d3 // 3
    nh = d // HEAD_LANES
    n_groups, spec = _attn_specs(d, s, DA_STREAMS)
    slopes = jnp.asarray([2.0 ** (-8.0 * (h + 1) / nh) for h in range(nh)], F32)
    grid_spec = pltpu.PrefetchScalarGridSpec(
        num_scalar_prefetch=1,
        grid=(b, n_groups),
        in_specs=[pl.BlockSpec(lam_vecs.shape, lambda *_: (0, 0)),
                  pl.BlockSpec((1, HEAD_LANES), lambda *_: (0, 0)),
                  spec(0), spec(1), spec(2)],
        out_specs=spec(0),
        scratch_shapes=[
            pltpu.VMEM((DA_STREAMS, 2 * ATT_BLOCK, HEAD_LANES), BF16),
            pltpu.VMEM((2 * DA_STREAMS, s // ATT_BLOCK, ATT_BLOCK, ATT_BLOCK), F32),
            pltpu.VMEM((2 * DA_STREAMS, ATT_BLOCK, HEAD_LANES), F32),
            pltpu.VMEM((2 * DA_STREAMS, ATT_BLOCK, HEAD_LANES), F32),
            pltpu.VMEM((2 * DA_STREAMS, ATT_BLOCK, HEAD_LANES), F32)],
    )
    return pl.pallas_call(
        functools.partial(_da_kernel, lambda_init=lambda_init),
        grid_spec=grid_spec,
        out_shape=jax.ShapeDtypeStruct((b, s, d), BF16),
        compiler_params=_params(2),
        name="diff_attention",
    )(slopes, lam_vecs, subln_g.reshape(1, HEAD_LANES), qkv, qkv, qkv)


def _sb_kernel(q_ref, k_ref, v_ref, o_ref, q2_ref, tail_ref, acc_ref):
    t = ATT_BLOCK
    n_blocks = q_ref.shape[1] // t
    streams = range(SB_STREAMS)
    cr = SB_UNIT_ROWS
    per_stream = 2 * t // cr
    units = [(s, c) for s in streams for c in range(per_stream)]

    row = lax.broadcasted_iota(jnp.int32, (2 * t, t), 0) % t
    col = lax.broadcasted_iota(jnp.int32, (2 * t, t), 1)
    strict = col < row
    strict_u = [strict[c * cr:(c + 1) * cr] for c in range(per_stream)]
    kr = lax.broadcasted_iota(jnp.int32, (t, t), 0)
    kc = lax.broadcasted_iota(jnp.int32, (t, t), 1)
    later = (kr > kc).astype(BF16)
    lane = lax.broadcasted_iota(jnp.int32, (t, HEAD_LANES), 1)

    def update(j, first):
        keys = pl.ds(pl.multiple_of(j * t, t), t)
        ws, mids = {}, {}

        def stage_scores(u):
            s, c = units[u]
            ws[u] = _dot_nt(q2_ref[s, c * cr:(c + 1) * cr, :], k_ref[0, keys, _lanes(s)]) * LOG2E

        def stage_suffix(u):
            w = ws.pop(u)
            log_beta = jnp.minimum(w, 0.0) - jnp.log2(1.0 + jnp.exp2(-jnp.abs(w)))
            log_not = log_beta - w
            if first:
                log_not = jnp.where(strict_u[units[u][1]], log_not, 0.0)
            suffix = _dot(log_not.astype(BF16), later)
            total = jnp.broadcast_to(suffix[:, 0:1] + log_not[:, 0:1], (cr, HEAD_LANES))
            mids[u] = (log_beta + suffix, total)

        def stage_values(u):
            s, c = units[u]
            pre, total = mids.pop(u)
            if first:
                a = jnp.where(strict_u[c], jnp.exp2(pre), 0.0)
                acc_ref[u] = _dot(a.astype(BF16), v_ref[0, keys, _lanes(s)])
                tail_ref[u] = total
            else:
                tail = tail_ref[u]
                a = jnp.exp2(pre + jnp.concatenate([tail] * (t // HEAD_LANES), axis=1))
                acc_ref[u] += _dot(a.astype(BF16), v_ref[0, keys, _lanes(s)])
                tail_ref[u] = tail + total

        n = len(units)
        for step in range(n + 2):
            if step < n:
                stage_scores(step)
            if 0 <= step - 1 < n:
                stage_suffix(step - 1)
            if 0 <= step - 2 < n:
                stage_values(step - 2)

    def q_block(i, carry):
        rows = pl.ds(pl.multiple_of(i * t, t), t)
        for s in streams:
            q2_ref[s] = _stack_halves(q_ref[0, rows, _lanes(s)])
        update(i, True)

        def k_block(n, c):
            update(i - 1 - n, False)
            return c

        lax.fori_loop(0, i, k_block, 0)
        for s in streams:
            acc = jnp.concatenate([acc_ref[s * per_stream + c] for c in range(per_stream)], axis=0)
            o_ref[0, rows, _lanes(s)] = jnp.where(lane < HALF, acc[:t], acc[t:]).astype(BF16)
        return carry

    lax.fori_loop(0, n_blocks, q_block, 0)


def _stick_breaking_attention(qkv):
    b, s, d3 = qkv.shape
    d = d3 // 3
    n_groups, spec = _attn_specs(d, s, SB_STREAMS)
    n_units = SB_STREAMS * 2 * ATT_BLOCK // SB_UNIT_ROWS
    return pl.pallas_call(
        _sb_kernel,
        grid=(b, n_groups),
        in_specs=[spec(0), spec(1), spec(2)],
        out_specs=spec(0),
        out_shape=jax.ShapeDtypeStruct((b, s, d), BF16),
        scratch_shapes=[pltpu.VMEM((SB_STREAMS, 2 * ATT_BLOCK, HEAD_LANES), BF16),
                        pltpu.VMEM((n_units, SB_UNIT_ROWS, HEAD_LANES), F32),
                        pltpu.VMEM((n_units, SB_UNIT_ROWS, HEAD_LANES), F32)],
        compiler_params=_params(2),
        name="stick_breaking_attention",
    )(qkv, qkv, qkv)


def _post_mixer_kernel(x_ref, a_ref, p_ref, g_ref, wo_ref, win_ref, wout_ref, wproj_ref,
                       wgate_ref, o_ref, h_ref, acc_ref):
    o_ref[...] = x_ref[...] + _rms(_dot(a_ref[...], wo_ref[...]), g_ref[0:1])
    o_ref[...] = _swiglu_half_step(o_ref[...], g_ref[1:2], g_ref[2:3],
                                   win_ref, wout_ref, h_ref, acc_ref)
    x = o_ref[...]
    e = _dot(p_ref[...].astype(BF16), wproj_ref[...])
    gate = jax.nn.sigmoid(_dot(_rms(x, g_ref[3:4]).astype(BF16), wgate_ref[...]))
    o_ref[...] = x + _rms(gate * e, g_ref[4:5])


def _post_mixer(x, attn, p, gains, w_o, w_in, w_out, w_proj, w_gate):
    n, d = x.shape
    dp = p.shape[1]
    row = pl.BlockSpec((FFN_ROW_TILE, d), lambda i: (i, 0))
    weights = [w.astype(BF16) for w in (w_o, w_in, w_out, w_proj, w_gate)]
    return pl.pallas_call(
        _post_mixer_kernel,
        grid=(n // FFN_ROW_TILE,),
        in_specs=[row, row, pl.BlockSpec((FFN_ROW_TILE, dp), lambda i: (i, 0)),
                  _resident(gains.shape)] + [_resident(w.shape) for w in weights],
        out_specs=row,
        out_shape=jax.ShapeDtypeStruct((n, d), F32),
        scratch_shapes=[pltpu.VMEM((FFN_ROW_TILE, d), BF16), pltpu.VMEM((FFN_ROW_TILE, d), F32)],
        compiler_params=_params(1),
        name="post_mixer",
    )(x, attn, p, gains, *weights)


def kernel(x, p, norm_gains, ffn1_w_in, ffn1_w_out, ffn2_w_in, ffn2_w_out, da_w_qkv, da_w_o,
           da_lambda, da_subln, sb_w_qkv, sb_w_o, ple_w_proj, ple_w_gate):
    b, s, d = x.shape
    depth = p.shape[0]
    n = b * s
    xf = x.reshape(n, d)
    for i in range(depth):
        g = norm_gains[i]
        j = i // 2
        xf = _ffn(xf, g[0], g[1], ffn1_w_in[i], ffn1_w_out[i])
        if i % 2 == 0:
            qkv = _qkv_proj(xf, g[2], da_w_qkv[j]).reshape(b, s, 3 * d)
            lambda_init = 0.8 - 0.6 * math.exp(-0.3 * i)
            attn = _diff_attention(qkv, da_lambda[j], da_subln[j], lambda_init)
            w_o = da_w_o[j]
        else:
            qkv = _qkv_proj(xf, g[2], sb_w_qkv[j]).reshape(b, s, 3 * d)
            attn = _stick_breaking_attention(qkv)
            w_o = sb_w_o[j]
        xf = _post_mixer(xf, attn.reshape(n, d), p[i].reshape(n, p.shape[-1]), g[3:8], w_o,
                         ffn2_w_in[i], ffn2_w_out[i], ple_w_proj[i], ple_w_gate[i])
    return xf.reshape(b, s, d)
```

```python
import functools
import math

import jax
import jax.numpy as jnp
from jax import lax
from jax.experimental import pallas as pl
from jax.experimental.pallas import tpu as pltpu

EPS = 1e-6
NEG_BIG = -1e30
LOG2E = math.log2(math.e)
CHUNK = 64
HEAD_LANES = 128
HALF = 64
QK_SCALE = 1.0 / math.sqrt(HALF)
ATT_BLOCK = 256
DA_STREAMS = 4
SB_STREAMS = 8
SB_UNIT_ROWS = 256
SB_MAX_LOG2 = 126.0
ROW_TILE = 512
FFN_ROW_TILE = 512
FF_CHUNK = 256
QKV_CHUNK = 512
VMEM_LIMIT = 56 * 1024 * 1024

F32 = jnp.float32
BF16 = jnp.bfloat16


def _rms(x, g):
    ms = jnp.mean(x * x, axis=-1, keepdims=True)
    return x * lax.rsqrt(ms + EPS) * g


def _dot(a, b):
    return lax.dot_general(a, b, (((1,), (0,)), ((), ())), preferred_element_type=F32)


def _dot_nt(a, b):
    return lax.dot_general(a, b, (((1,), (1,)), ((), ())), preferred_element_type=F32)


def _resident(shape):
    zeros = (0,) * len(shape)
    return pl.BlockSpec(shape, lambda *_: zeros, pipeline_mode=pl.Buffered(1))


def _params(n_axes):
    return pltpu.CompilerParams(
        dimension_semantics=("arbitrary",) * n_axes, vmem_limit_bytes=VMEM_LIMIT)


def _swiglu_half_step(x, gpre, gpost, win_ref, wout_ref, h_ref, acc_ref):
    h_ref[...] = _rms(x, gpre).astype(BF16)
    acc_ref[...] = jnp.zeros_like(acc_ref)

    f = wout_ref.shape[0]
    for j in range(f // FF_CHUNK):
        lo = j * FF_CHUNK
        h = h_ref[...]
        a = _dot(h, win_ref[:, lo:lo + FF_CHUNK])
        b = _dot(h, win_ref[:, f + lo:f + lo + FF_CHUNK])
        g = (a * jax.nn.sigmoid(a) * b).astype(BF16)
        acc_ref[...] += _dot(g, wout_ref[lo:lo + FF_CHUNK, :])
    return x + 0.5 * _rms(acc_ref[...], gpost)


def _ffn_kernel(x_ref, gpre_ref, gpost_ref, win_ref, wout_ref, o_ref, h_ref, acc_ref):
    o_ref[...] = _swiglu_half_step(x_ref[...], gpre_ref[...], gpost_ref[...],
                                   win_ref, wout_ref, h_ref, acc_ref)


def _ffn(x, g_pre, g_post, w_in, w_out):
    n, d = x.shape
    win = w_in.astype(BF16)
    wout = w_out.astype(BF16)
    row = pl.BlockSpec((FFN_ROW_TILE, d), lambda i: (i, 0))
    return pl.pallas_call(
        _ffn_kernel,
        grid=(n // FFN_ROW_TILE,),
        in_specs=[row, _resident((1, d)), _resident((1, d)),
                  _resident(win.shape), _resident(wout.shape)],
        out_specs=row,
        out_shape=jax.ShapeDtypeStruct((n, d), F32),
        scratch_shapes=[pltpu.VMEM((FFN_ROW_TILE, d), BF16), pltpu.VMEM((FFN_ROW_TILE, d), F32)],
        compiler_params=_params(1),
        name="ffn",
    )(x, g_pre.reshape(1, d), g_post.reshape(1, d), win, wout)


def _qkv_kernel(x_ref, g_ref, w_ref, o_ref):
    h = _rms(x_ref[...], g_ref[...]).astype(BF16)
    for c in range(w_ref.shape[1] // QKV_CHUNK):
        cols = slice(c * QKV_CHUNK, (c + 1) * QKV_CHUNK)
        o_ref[:, cols] = _dot(h, w_ref[:, cols]).astype(BF16)


def _qkv_proj(x, g, w_qkv):
    n, d = x.shape
    d3 = w_qkv.shape[1]
    return pl.pallas_call(
        _qkv_kernel,
        grid=(n // ROW_TILE,),
        in_specs=[pl.BlockSpec((ROW_TILE, d), lambda i: (i, 0)),
                  _resident((1, d)), _resident((d, d3))],
        out_specs=pl.BlockSpec((ROW_TILE, d3), lambda i: (i, 0)),
        out_shape=jax.ShapeDtypeStruct((n, d3), BF16),
        compiler_params=_params(1),
        name="qkv_proj",
    )(x, g.reshape(1, d), w_qkv.astype(BF16))


def _stack_halves(q):
    lane = lax.broadcasted_iota(jnp.int32, q.shape, 1)
    qs = q.astype(F32) * QK_SCALE
    lo = jnp.where(lane < HALF, qs, 0.0)
    hi = jnp.where(lane >= HALF, qs, 0.0)
    return jnp.concatenate([lo, hi], axis=0).astype(BF16)


def _attn_specs(d, seq, n_streams):
    width = n_streams * HEAD_LANES
    n_groups = d // width

    def spec(third):
        return pl.BlockSpec((1, seq, width), lambda b, h, *_: (b, 0, third * n_groups + h))
    return n_groups, spec


def _lanes(s):
    return slice(s * HEAD_LANES, (s + 1) * HEAD_LANES)


def _da_kernel(slope_ref, lamv_ref, subln_ref, q_ref, k_ref, v_ref, o_ref,
               q2_ref, vones_ref, s_ref, mx_ref, acc_ref, *, lambda_init):
    t = ATT_BLOCK
    n_blocks = q_ref.shape[1] // t
    streams = range(DA_STREAMS)
    slopes = [slope_ref[pl.program_id(1) * DA_STREAMS + s] for s in streams]

    for s in streams:
        vones_ref[s, :, :HEAD_LANES] = v_ref[0, :, _lanes(s)]
        vones_ref[s, :, HEAD_LANES:] = jnp.ones((v_ref.shape[1], HEAD_LANES), BF16)

    lv = lamv_ref[...]
    lam = (jnp.exp(jnp.sum(lv[0:1] * lv[1:2], axis=1, keepdims=True))
           - jnp.exp(jnp.sum(lv[2:3] * lv[3:4], axis=1, keepdims=True)) + lambda_init)
    subln = subln_ref[...]

    row = lax.broadcasted_iota(jnp.int32, (t, t), 0)
    col = lax.broadcasted_iota(jnp.int32, (t, t), 1)
    diag_shape = (row - jnp.abs(row - col)).astype(F32)
    allowed = col // CHUNK <= row // CHUNK
    diag_bias = [jnp.where(allowed, slopes[s] * diag_shape, NEG_BIG) for s in streams]
    col_f = lax.broadcasted_iota(jnp.int32, (1, t), 1).astype(F32)

    units = [(s, c) for s in streams for c in range(2)]
    n_lane_tiles = t // HEAD_LANES

    def lane_fold(x, op):
        out = x[:, :HEAD_LANES]
        for c in range(1, n_lane_tiles):
            out = op(out, x[:, c * HEAD_LANES:(c + 1) * HEAD_LANES])
        return out

    def score_pass(j, bias, first):
        keys = pl.ds(pl.multiple_of(j * t, t), t)
        raw = {}

        def matmul(u):
            s, c = units[u]
            raw[u] = _dot_nt(q2_ref[s, c * t:(c + 1) * t, :], k_ref[0, keys, _lanes(s)])

        def finish(u):
            sc = raw.pop(u) + bias[units[u][0]]
            s_ref[u, j] = sc
            tile_max = lane_fold(sc, jnp.maximum)
            mx_ref[u] = tile_max if first else jnp.maximum(mx_ref[u], tile_max)

        for step in range(len(units) + 1):
            if step < len(units):
                matmul(step)
            if step >= 1:
                finish(step - 1)

    def value_pass(j, first):
        keys = pl.ds(pl.multiple_of(j * t, t), t)
        for u, (s, c) in enumerate(units):
            m = mx_ref[u]
            p = jnp.exp(s_ref[u, j] - jnp.concatenate([m] * n_lane_tiles, axis=1))
            pv = _dot(p.astype(BF16), vones_ref[s, keys, :])
            if first:
                acc_ref[u] = pv
            else:
                acc_ref[u] += pv

    def q_block(i, carry):
        rows = pl.ds(pl.multiple_of(i * t, t), t)
        for s in streams:
            q2_ref[s] = _stack_halves(q_ref[0, rows, _lanes(s)])

        score_pass(i, diag_bias, True)

        def earlier_scores(j, c):
            back = jnp.asarray((i - j) * t, F32)
            score_pass(j, [slopes[s] * (col_f - back) for s in streams], False)
            return c

        lax.fori_loop(0, i, earlier_scores, 0)
        for u in range(len(units)):
            row_max = jnp.max(mx_ref[u], axis=1, keepdims=True)
            mx_ref[u] = jnp.broadcast_to(row_max, (t, HEAD_LANES))

        value_pass(0, True)

        def later_values(j, c):
            value_pass(j, False)
            return c

        lax.fori_loop(1, i + 1, later_values, 0)
        for s in streams:
            pv0, pv1 = acc_ref[2 * s], acc_ref[2 * s + 1]
            o = (pv0[:, :HEAD_LANES] / pv0[:, HEAD_LANES:]
                 - lam * (pv1[:, :HEAD_LANES] / pv1[:, HEAD_LANES:]))
            o = _rms(o, subln) * (1.0 - lambda_init)
            o_ref[0, rows, _lanes(s)] = o.astype(BF16)
        return carry

    lax.fori_loop(0, n_blocks, q_block, 0)


def _diff_attention(qkv, lam_vecs, subln_g, lambda_init):
    b, s, d3 = qkv.shape
    d = d3 // 3
    nh = d // HEAD_LANES
    n_groups, spec = _attn_specs(d, s, DA_STREAMS)
    slopes = jnp.asarray([2.0 ** (-8.0 * (h + 1) / nh) for h in range(nh)], F32)
    grid_spec = pltpu.PrefetchScalarGridSpec(
        num_scalar_prefetch=1,
        grid=(b, n_groups),
        in_specs=[pl.BlockSpec(lam_vecs.shape, lambda *_: (0, 0)),
                  pl.BlockSpec((1, HEAD_LANES), lambda *_: (0, 0)),
                  spec(0), spec(1), spec(2)],
        out_specs=spec(0),
        scratch_shapes=[
            pltpu.VMEM((DA_STREAMS, 2 * ATT_BLOCK, HEAD_LANES), BF16),
            pltpu.VMEM((DA_STREAMS, s, 2 * HEAD_LANES), BF16),
            pltpu.VMEM((2 * DA_STREAMS, s // ATT_BLOCK, ATT_BLOCK, ATT_BLOCK), F32),
            pltpu.VMEM((2 * DA_STREAMS, ATT_BLOCK, HEAD_LANES), F32),
            pltpu.VMEM((2 * DA_STREAMS, ATT_BLOCK, 2 * HEAD_LANES), F32)],
    )
    return pl.pallas_call(
        functools.partial(_da_kernel, lambda_init=lambda_init),
        grid_spec=grid_spec,
        out_shape=jax.ShapeDtypeStruct((b, s, d), BF16),
        compiler_params=_params(2),
        name="diff_attention",
    )(slopes, lam_vecs, subln_g.reshape(1, HEAD_LANES), qkv, qkv, qkv)


def _sb_kernel(q_ref, k_ref, v_ref, o_ref, q2_ref, tail_ref, acc_ref):
    t = ATT_BLOCK
    n_blocks = q_ref.shape[1] // t
    streams = range(SB_STREAMS)
    cr = SB_UNIT_ROWS
    per_stream = 2 * t // cr
    units = [(s, c) for s in streams for c in range(per_stream)]
    n_units = len(units)

    row = lax.broadcasted_iota(jnp.int32, (2 * t, t), 0) % t
    col = lax.broadcasted_iota(jnp.int32, (2 * t, t), 1)
    strict = col < row
    strict_u = [strict[c * cr:(c + 1) * cr] for c in range(per_stream)]
    kr = lax.broadcasted_iota(jnp.int32, (t, t), 0)
    kc = lax.broadcasted_iota(jnp.int32, (t, t), 1)
    later = (kr > kc).astype(BF16)
    lane = lax.broadcasted_iota(jnp.int32, (t, HEAD_LANES), 1)

    def update(j, first):
        keys = pl.ds(pl.multiple_of(j * t, t), t)
        ws, mids = {}, {}

        def stage_scores(u):
            s, c = units[u]
            w = _dot_nt(q2_ref[s, c * cr:(c + 1) * cr, :], k_ref[0, keys, _lanes(s)]) * LOG2E
            ws[u] = jnp.minimum(w, SB_MAX_LOG2)

        def stage_costs(u):
            w = ws.pop(u)
            cost = jnp.log2(1.0 + jnp.exp2(w))
            log_beta = w - cost
            if first:
                cost = jnp.where(strict_u[units[u][1]], cost, 0.0)
            sums = _dot(cost.astype(BF16), later)
            total = jnp.broadcast_to(sums[:, 0:1] + cost[:, 0:1], (cr, HEAD_LANES))
            mids[u] = (log_beta - sums, total)

        def stage_values(u):
            s, c = units[u]
            pre, total = mids.pop(u)
            if first:
                a = jnp.where(strict_u[c], jnp.exp2(pre), 0.0)
                acc_ref[u] = _dot(a.astype(BF16), v_ref[0, keys, _lanes(s)])
                tail_ref[u] = total
            else:
                tail = tail_ref[u]
                a = jnp.exp2(pre - jnp.concatenate([tail] * (t // HEAD_LANES), axis=1))
                acc_ref[u] += _dot(a.astype(BF16), v_ref[0, keys, _lanes(s)])
                tail_ref[u] = tail + total

        for step in range(n_units + 2):
            if step < n_units:
                stage_scores(step)
            if 0 <= step - 1 < n_units:
                stage_costs(step - 1)
            if 0 <= step - 2 < n_units:
                stage_values(step - 2)

    def q_block(i, carry):
        rows = pl.ds(pl.multiple_of(i * t, t), t)
        for s in streams:
            q2_ref[s] = _stack_halves(q_ref[0, rows, _lanes(s)])
        update(i, True)

        def k_block(n, c):
            update(i - 1 - n, False)
            return c

        lax.fori_loop(0, i, k_block, 0)
        for s in streams:
            acc = jnp.concatenate([acc_ref[s * per_stream + c] for c in range(per_stream)], axis=0)
            o_ref[0, rows, _lanes(s)] = jnp.where(lane < HALF, acc[:t], acc[t:]).astype(BF16)
        return carry

    lax.fori_loop(0, n_blocks, q_block, 0)


def _stick_breaking_attention(qkv):
    b, s, d3 = qkv.shape
    d = d3 // 3
    n_groups, spec = _attn_specs(d, s, SB_STREAMS)
    n_units = SB_STREAMS * 2 * ATT_BLOCK // SB_UNIT_ROWS
    return pl.pallas_call(
        _sb_kernel,
        grid=(b, n_groups),
        in_specs=[spec(0), spec(1), spec(2)],
        out_specs=spec(0),
        out_shape=jax.ShapeDtypeStruct((b, s, d), BF16),
        scratch_shapes=[pltpu.VMEM((SB_STREAMS, 2 * ATT_BLOCK, HEAD_LANES), BF16),
                        pltpu.VMEM((n_units, SB_UNIT_ROWS, HEAD_LANES), F32),
                        pltpu.VMEM((n_units, SB_UNIT_ROWS, HEAD_LANES), F32)],
        compiler_params=_params(2),
        name="stick_breaking_attention",
    )(qkv, qkv, qkv)


def _post_mixer_kernel(x_ref, a_ref, p_ref, g_ref, wo_ref, win_ref, wout_ref, wproj_ref,
                       wgate_ref, o_ref, h_ref, acc_ref):
    o_ref[...] = x_ref[...] + _rms(_dot(a_ref[...], wo_ref[...]), g_ref[0:1])
    o_ref[...] = _swiglu_half_step(o_ref[...], g_ref[1:2], g_ref[2:3],
                                   win_ref, wout_ref, h_ref, acc_ref)
    x = o_ref[...]
    e = _dot(p_ref[...].astype(BF16), wproj_ref[...])
    gate = jax.nn.sigmoid(_dot(_rms(x, g_ref[3:4]).astype(BF16), wgate_ref[...]))
    o_ref[...] = x + _rms(gate * e, g_ref[4:5])


def _post_mixer(x, attn, p, gains, w_o, w_in, w_out, w_proj, w_gate):
    n, d = x.shape
    dp = p.shape[1]
    row = pl.BlockSpec((FFN_ROW_TILE, d), lambda i: (i, 0))
    weights = [w.astype(BF16) for w in (w_o, w_in, w_out, w_proj, w_gate)]
    return pl.pallas_call(
        _post_mixer_kernel,
        grid=(n // FFN_ROW_TILE,),
        in_specs=[row, row, pl.BlockSpec((FFN_ROW_TILE, dp), lambda i: (i, 0)),
                  _resident(gains.shape)] + [_resident(w.shape) for w in weights],
        out_specs=row,
        out_shape=jax.ShapeDtypeStruct((n, d), F32),
        scratch_shapes=[pltpu.VMEM((FFN_ROW_TILE, d), BF16), pltpu.VMEM((FFN_ROW_TILE, d), F32)],
        compiler_params=_params(1),
        name="post_mixer",
    )(x, attn, p, gains, *weights)


def kernel(x, p, norm_gains, ffn1_w_in, ffn1_w_out, ffn2_w_in, ffn2_w_out, da_w_qkv, da_w_o,
           da_lambda, da_subln, sb_w_qkv, sb_w_o, ple_w_proj, ple_w_gate):
    b, s, d = x.shape
    depth = p.shape[0]
    n = b * s
    xf = x.reshape(n, d)
    for i in range(depth):
        g = norm_gains[i]
        j = i // 2
        xf = _ffn(xf, g[0], g[1], ffn1_w_in[i], ffn1_w_out[i])
        if i % 2 == 0:
            qkv = _qkv_proj(xf, g[2], da_w_qkv[j]).reshape(b, s, 3 * d)
            lambda_init = 0.8 - 0.6 * math.exp(-0.3 * i)
            attn = _diff_attention(qkv, da_lambda[j], da_subln[j], lambda_init)
            w_o = da_w_o[j]
        else:
            qkv = _qkv_proj(xf, g[2], sb_w_qkv[j]).reshape(b, s, 3 * d)
            attn = _stick_breaking_attention(qkv)
            w_o = sb_w_o[j]
        xf = _post_mixer(xf, attn.reshape(n, d), p[i].reshape(n, p.shape[-1]), g[3:8], w_o,
                         ffn2_w_in[i], ffn2_w_out[i], ple_w_proj[i], ple_w_gate[i])
    return xf.reshape(b, s, d)
```

```python
import functools
import math

import jax
import jax.numpy as jnp
from jax import lax
from jax.experimental import pallas as pl
from jax.experimental.pallas import tpu as pltpu

EPS = 1e-6
NEG_BIG = -1e30
LOG2E = math.log2(math.e)
CHUNK = 64
HEAD_LANES = 128
HALF = 64
QK_SCALE = 1.0 / math.sqrt(HALF)
ATT_BLOCK = 256
DA_STREAMS = 4
SB_STREAMS = 8
SB_UNIT_ROWS = 256
SB_MAX_LOG2 = 126.0
ROW_TILE = 512
FFN_ROW_TILE = 512
FF_CHUNK = 256
QKV_CHUNK = 512
VMEM_LIMIT = 56 * 1024 * 1024

F32 = jnp.float32
BF16 = jnp.bfloat16


def _rms(x, g):
    ms = jnp.mean(x * x, axis=-1, keepdims=True)
    return x * lax.rsqrt(ms + EPS) * g


def _dot(a, b):
    return lax.dot_general(a, b, (((1,), (0,)), ((), ())), preferred_element_type=F32)


def _dot_nt(a, b):
    return lax.dot_general(a, b, (((1,), (1,)), ((), ())), preferred_element_type=F32)


def _resident(shape):
    zeros = (0,) * len(shape)
    return pl.BlockSpec(shape, lambda *_: zeros, pipeline_mode=pl.Buffered(1))


def _params(n_axes):
    return pltpu.CompilerParams(
        dimension_semantics=("arbitrary",) * n_axes, vmem_limit_bytes=VMEM_LIMIT)


def _swiglu_half_step(x, gpre, gpost, win_ref, wout_ref, h_ref, acc_ref):
    h_ref[...] = _rms(x, gpre).astype(BF16)
    acc_ref[...] = jnp.zeros_like(acc_ref)

    f = wout_ref.shape[0]
    for j in range(f // FF_CHUNK):
        lo = j * FF_CHUNK
        h = h_ref[...]
        a = _dot(h, win_ref[:, lo:lo + FF_CHUNK])
        b = _dot(h, win_ref[:, f + lo:f + lo + FF_CHUNK])
        g = (a * jax.nn.sigmoid(a) * b).astype(BF16)
        acc_ref[...] += _dot(g, wout_ref[lo:lo + FF_CHUNK, :])
    return x + 0.5 * _rms(acc_ref[...], gpost)


def _ffn_kernel(x_ref, gpre_ref, gpost_ref, win_ref, wout_ref, o_ref, h_ref, acc_ref):
    o_ref[...] = _swiglu_half_step(x_ref[...], gpre_ref[...], gpost_ref[...],
                                   win_ref, wout_ref, h_ref, acc_ref)


def _ffn(x, g_pre, g_post, w_in, w_out):
    n, d = x.shape
    win = w_in.astype(BF16)
    wout = w_out.astype(BF16)
    row = pl.BlockSpec((FFN_ROW_TILE, d), lambda i: (i, 0))
    return pl.pallas_call(
        _ffn_kernel,
        grid=(n // FFN_ROW_TILE,),
        in_specs=[row, _resident((1, d)), _resident((1, d)),
                  _resident(win.shape), _resident(wout.shape)],
        out_specs=row,
        out_shape=jax.ShapeDtypeStruct((n, d), F32),
        scratch_shapes=[pltpu.VMEM((FFN_ROW_TILE, d), BF16), pltpu.VMEM((FFN_ROW_TILE, d), F32)],
        compiler_params=_params(1),
        name="ffn",
    )(x, g_pre.reshape(1, d), g_post.reshape(1, d), win, wout)


def _qkv_kernel(x_ref, g_ref, w_ref, o_ref):
    h = _rms(x_ref[...], g_ref[...]).astype(BF16)
    for c in range(w_ref.shape[1] // QKV_CHUNK):
        cols = slice(c * QKV_CHUNK, (c + 1) * QKV_CHUNK)
        o_ref[:, cols] = _dot(h, w_ref[:, cols]).astype(BF16)


def _qkv_proj(x, g, w_qkv):
    n, d = x.shape
    d3 = w_qkv.shape[1]
    return pl.pallas_call(
        _qkv_kernel,
        grid=(n // ROW_TILE,),
        in_specs=[pl.BlockSpec((ROW_TILE, d), lambda i: (i, 0)),
                  _resident((1, d)), _resident((d, d3))],
        out_specs=pl.BlockSpec((ROW_TILE, d3), lambda i: (i, 0)),
        out_shape=jax.ShapeDtypeStruct((n, d3), BF16),
        compiler_params=_params(1),
        name="qkv_proj",
    )(x, g.reshape(1, d), w_qkv.astype(BF16))


def _stack_halves(q):
    lane = lax.broadcasted_iota(jnp.int32, q.shape, 1)
    qs = q.astype(F32) * QK_SCALE
    lo = jnp.where(lane < HALF, qs, 0.0)
    hi = jnp.where(lane >= HALF, qs, 0.0)
    return jnp.concatenate([lo, hi], axis=0).astype(BF16)


def _attn_specs(d, seq, n_streams):
    width = n_streams * HEAD_LANES
    n_groups = d // width

    def spec(third):
        return pl.BlockSpec((1, seq, width), lambda b, h, *_: (b, 0, third * n_groups + h))
    return n_groups, spec


def _lanes(s):
    return slice(s * HEAD_LANES, (s + 1) * HEAD_LANES)


def _da_kernel(slope_ref, lamv_ref, subln_ref, q_ref, k_ref, v_ref, o_ref,
               q2_ref, vones_ref, s_ref, mx_ref, acc_ref, *, lambda_init):
    t = ATT_BLOCK
    n_blocks = q_ref.shape[1] // t
    streams = range(DA_STREAMS)
    slopes = [slope_ref[pl.program_id(1) * DA_STREAMS + s] for s in streams]

    for s in streams:
        vones_ref[s, :, :HEAD_LANES] = v_ref[0, :, _lanes(s)]
        vones_ref[s, :, HEAD_LANES:] = jnp.ones((v_ref.shape[1], HEAD_LANES), BF16)

    lv = lamv_ref[...]
    lam = (jnp.exp(jnp.sum(lv[0:1] * lv[1:2], axis=1, keepdims=True))
           - jnp.exp(jnp.sum(lv[2:3] * lv[3:4], axis=1, keepdims=True)) + lambda_init)
    subln = subln_ref[...]

    row = lax.broadcasted_iota(jnp.int32, (t, t), 0)
    col = lax.broadcasted_iota(jnp.int32, (t, t), 1)
    diag_shape = (row - jnp.abs(row - col)).astype(F32)
    allowed = col // CHUNK <= row // CHUNK
    diag_bias = [jnp.where(allowed, slopes[s] * diag_shape, NEG_BIG) for s in streams]
    col_f = lax.broadcasted_iota(jnp.int32, (1, t), 1).astype(F32)

    units = [(s, c) for s in streams for c in range(2)]
    n_lane_tiles = t // HEAD_LANES

    def lane_fold(x, op):
        out = x[:, :HEAD_LANES]
        for c in range(1, n_lane_tiles):
            out = op(out, x[:, c * HEAD_LANES:(c + 1) * HEAD_LANES])
        return out

    def key_rows(j):
        return pl.ds(pl.multiple_of(j * t, t), t)

    def score_pass(tiles, first):
        work = [(j, bias, u) for j, bias in tiles for u in range(len(units))]
        raw = {}

        def matmul(n):
            j, _, u = work[n]
            s, c = units[u]
            raw[n] = _dot_nt(q2_ref[s, c * t:(c + 1) * t, :], k_ref[0, key_rows(j), _lanes(s)])

        def finish(n):
            j, bias, u = work[n]
            sc = raw.pop(n) + bias[units[u][0]]
            s_ref[u, j] = sc
            tile_max = lane_fold(sc, jnp.maximum)
            mx_ref[u] = tile_max if first else jnp.maximum(mx_ref[u], tile_max)

        for step in range(len(work) + 1):
            if step < len(work):
                matmul(step)
            if step >= 1:
                finish(step - 1)

    def value_pass(tiles, first):
        for u, (s, c) in enumerate(units):
            m = jnp.concatenate([mx_ref[u]] * n_lane_tiles, axis=1)
            pv = None
            for j in tiles:
                p = jnp.exp(s_ref[u, j] - m)
                part = _dot(p.astype(BF16), vones_ref[s, key_rows(j), :])
                pv = part if pv is None else pv + part
            if first:
                acc_ref[u] = pv
            else:
                acc_ref[u] += pv

    def q_block(i, carry):
        rows = pl.ds(pl.multiple_of(i * t, t), t)
        for s in streams:
            q2_ref[s] = _stack_halves(q_ref[0, rows, _lanes(s)])

        def earlier(j):
            back = jnp.asarray((i - j) * t, F32)
            return j, [slopes[s] * (col_f - back) for s in streams]

        score_pass([(i, diag_bias)], True)

        def score_pair(n, c):
            score_pass([earlier(2 * n), earlier(2 * n + 1)], False)
            return c

        lax.fori_loop(0, i // 2, score_pair, 0)

        @pl.when(i % 2 == 1)
        def _():
            score_pass([earlier(i - 1)], False)

        for u in range(len(units)):
            row_max = jnp.max(mx_ref[u], axis=1, keepdims=True)
            mx_ref[u] = jnp.broadcast_to(row_max, (t, HEAD_LANES))

        value_pass([0], True)

        def value_pair(n, c):
            value_pass([2 * n + 1, 2 * n + 2], False)
            return c

        lax.fori_loop(0, i // 2, value_pair, 0)

        @pl.when(i % 2 == 1)
        def _():
            value_pass([i], False)

        for s in streams:
            pv0, pv1 = acc_ref[2 * s], acc_ref[2 * s + 1]
            o = (pv0[:, :HEAD_LANES] / pv0[:, HEAD_LANES:]
                 - lam * (pv1[:, :HEAD_LANES] / pv1[:, HEAD_LANES:]))
            o = _rms(o, subln) * (1.0 - lambda_init)
            o_ref[0, rows, _lanes(s)] = o.astype(BF16)
        return carry

    lax.fori_loop(0, n_blocks, q_block, 0)


def _diff_attention(qkv, lam_vecs, subln_g, lambda_init):
    b, s, d3 = qkv.shape
    d = d3 // 3
    nh = d // HEAD_LANES
    n_groups, spec = _attn_specs(d, s, DA_STREAMS)
    slopes = jnp.asarray([2.0 ** (-8.0 * (h + 1) / nh) for h in range(nh)], F32)
    grid_spec = pltpu.PrefetchScalarGridSpec(
        num_scalar_prefetch=1,
        grid=(b, n_groups),
        in_specs=[pl.BlockSpec(lam_vecs.shape, lambda *_: (0, 0)),
                  pl.BlockSpec((1, HEAD_LANES), lambda *_: (0, 0)),
                  spec(0), spec(1), spec(2)],
        out_specs=spec(0),
        scratch_shapes=[
            pltpu.VMEM((DA_STREAMS, 2 * ATT_BLOCK, HEAD_LANES), BF16),
            pltpu.VMEM((DA_STREAMS, s, 2 * HEAD_LANES), BF16),
            pltpu.VMEM((2 * DA_STREAMS, s // ATT_BLOCK, ATT_BLOCK, ATT_BLOCK), F32),
            pltpu.VMEM((2 * DA_STREAMS, ATT_BLOCK, HEAD_LANES), F32),
            pltpu.VMEM((2 * DA_STREAMS, ATT_BLOCK, 2 * HEAD_LANES), F32)],
    )
    return pl.pallas_call(
        functools.partial(_da_kernel, lambda_init=lambda_init),
        grid_spec=grid_spec,
        out_shape=jax.ShapeDtypeStruct((b, s, d), BF16),
        compiler_params=_params(2),
        name="diff_attention",
    )(slopes, lam_vecs, subln_g.reshape(1, HEAD_LANES), qkv, qkv, qkv)


def _sb_kernel(q_ref, k_ref, v_ref, o_ref, q2_ref, tail_ref, acc_ref):
    t = ATT_BLOCK
    n_blocks = q_ref.shape[1] // t
    streams = range(SB_STREAMS)
    cr = SB_UNIT_ROWS
    per_stream = 2 * t // cr
    units = [(s, c) for s in streams for c in range(per_stream)]
    n_units = len(units)

    row = lax.broadcasted_iota(jnp.int32, (2 * t, t), 0) % t
    col = lax.broadcasted_iota(jnp.int32, (2 * t, t), 1)
    strict = col < row
    strict_u = [strict[c * cr:(c + 1) * cr] for c in range(per_stream)]
    kr = lax.broadcasted_iota(jnp.int32, (t, t), 0)
    kc = lax.broadcasted_iota(jnp.int32, (t, t), 1)
    later = (kr > kc).astype(BF16)
    lane = lax.broadcasted_iota(jnp.int32, (t, HEAD_LANES), 1)

    def update(tiles, first):
        work = [(j, u) for j in tiles for u in range(n_units)]
        ws, mids = {}, {}

        def keys(n):
            return pl.ds(pl.multiple_of(work[n][0] * t, t), t)

        def stage_scores(n):
            s, c = units[work[n][1]]
            w = _dot_nt(q2_ref[s, c * cr:(c + 1) * cr, :], k_ref[0, keys(n), _lanes(s)]) * LOG2E
            ws[n] = jnp.minimum(w, SB_MAX_LOG2)

        def stage_costs(n):
            w = ws.pop(n)
            cost = jnp.log2(1.0 + jnp.exp2(w))
            log_beta = w - cost
            if first:
                cost = jnp.where(strict_u[units[work[n][1]][1]], cost, 0.0)
            sums = _dot(cost.astype(BF16), later)
            total = jnp.broadcast_to(sums[:, 0:1] + cost[:, 0:1], (cr, HEAD_LANES))
            mids[n] = (log_beta - sums, total)

        def stage_values(n):
            u = work[n][1]
            s, c = units[u]
            pre, total = mids.pop(n)
            if first:
                a = jnp.where(strict_u[c], jnp.exp2(pre), 0.0)
                acc_ref[u] = _dot(a.astype(BF16), v_ref[0, keys(n), _lanes(s)])
                tail_ref[u] = total
            else:
                tail = tail_ref[u]
                a = jnp.exp2(pre - jnp.concatenate([tail] * (t // HEAD_LANES), axis=1))
                acc_ref[u] += _dot(a.astype(BF16), v_ref[0, keys(n), _lanes(s)])
                tail_ref[u] = tail + total

        for step in range(len(work) + 2):
            if step < len(work):
                stage_scores(step)
            if 0 <= step - 1 < len(work):
                stage_costs(step - 1)
            if 0 <= step - 2 < len(work):
                stage_values(step - 2)

    def q_block(i, carry):
        rows = pl.ds(pl.multiple_of(i * t, t), t)
        for s in streams:
            q2_ref[s] = _stack_halves(q_ref[0, rows, _lanes(s)])
        update([i], True)

        def k_block(n, c):
            update([i - 1 - n], False)
            return c

        lax.fori_loop(0, i, k_block, 0)
        for s in streams:
            acc = jnp.concatenate([acc_ref[s * per_stream + c] for c in range(per_stream)], axis=0)
            o_ref[0, rows, _lanes(s)] = jnp.where(lane < HALF, acc[:t], acc[t:]).astype(BF16)
        return carry

    lax.fori_loop(0, n_blocks, q_block, 0)


def _stick_breaking_attention(qkv):
    b, s, d3 = qkv.shape
    d = d3 // 3
    n_groups, spec = _attn_specs(d, s, SB_STREAMS)
    n_units = SB_STREAMS * 2 * ATT_BLOCK // SB_UNIT_ROWS
    return pl.pallas_call(
        _sb_kernel,
        grid=(b, n_groups),
        in_specs=[spec(0), spec(1), spec(2)],
        out_specs=spec(0),
        out_shape=jax.ShapeDtypeStruct((b, s, d), BF16),
        scratch_shapes=[pltpu.VMEM((SB_STREAMS, 2 * ATT_BLOCK, HEAD_LANES), BF16),
                        pltpu.VMEM((n_units, SB_UNIT_ROWS, HEAD_LANES), F32),
                        pltpu.VMEM((n_units, SB_UNIT_ROWS, HEAD_LANES), F32)],
        compiler_params=_params(2),
        name="stick_breaking_attention",
    )(qkv, qkv, qkv)


def _post_mixer_kernel(x_ref, a_ref, p_ref, g_ref, wo_ref, win_ref, wout_ref, wproj_ref,
                       wgate_ref, o_ref, h_ref, acc_ref):
    o_ref[...] = x_ref[...] + _rms(_dot(a_ref[...], wo_ref[...]), g_ref[0:1])
    o_ref[...] = _swiglu_half_step(o_ref[...], g_ref[1:2], g_ref[2:3],
                                   win_ref, wout_ref, h_ref, acc_ref)
    x = o_ref[...]
    e = _dot(p_ref[...].astype(BF16), wproj_ref[...])
    gate = jax.nn.sigmoid(_dot(_rms(x, g_ref[3:4]).astype(BF16), wgate_ref[...]))
    o_ref[...] = x + _rms(gate * e, g_ref[4:5])


def _post_mixer(x, attn, p, gains, w_o, w_in, w_out, w_proj, w_gate):
    n, d = x.shape
    dp = p.shape[1]
    row = pl.BlockSpec((FFN_ROW_TILE, d), lambda i: (i, 0))
    weights = [w.astype(BF16) for w in (w_o, w_in, w_out, w_proj, w_gate)]
    return pl.pallas_call(
        _post_mixer_kernel,
        grid=(n // FFN_ROW_TILE,),
        in_specs=[row, row, pl.BlockSpec((FFN_ROW_TILE, dp), lambda i: (i, 0)),
                  _resident(gains.shape)] + [_resident(w.shape) for w in weights],
        out_specs=row,
        out_shape=jax.ShapeDtypeStruct((n, d), F32),
        scratch_shapes=[pltpu.VMEM((FFN_ROW_TILE, d), BF16), pltpu.VMEM((FFN_ROW_TILE, d), F32)],
        compiler_params=_params(1),
        name="post_mixer",
    )(x, attn, p, gains, *weights)


def kernel(x, p, norm_gains, ffn1_w_in, ffn1_w_out, ffn2_w_in, ffn2_w_out, da_w_qkv, da_w_o,
           da_lambda, da_subln, sb_w_qkv, sb_w_o, ple_w_proj, ple_w_gate):
    b, s, d = x.shape
    depth = p.shape[0]
    n = b * s
    xf = x.reshape(n, d)
    for i in range(depth):
        g = norm_gains[i]
        j = i // 2
        xf = _ffn(xf, g[0], g[1], ffn1_w_in[i], ffn1_w_out[i])
        if i % 2 == 0:
            qkv = _qkv_proj(xf, g[2], da_w_qkv[j]).reshape(b, s, 3 * d)
            lambda_init = 0.8 - 0.6 * math.exp(-0.3 * i)
            attn = _diff_attention(qkv, da_lambda[j], da_subln[j], lambda_init)
            w_o = da_w_o[j]
        else:
            qkv = _qkv_proj(xf, g[2], sb_w_qkv[j]).reshape(b, s, 3 * d)
            attn = _stick_breaking_attention(qkv)
            w_o = sb_w_o[j]
        xf = _post_mixer(xf, attn.reshape(n, d), p[i].reshape(n, p.shape[-1]), g[3:8], w_o,
                         ffn2_w_in[i], ffn2_w_out[i], ple_w_proj[i], ple_w_gate[i])
    return xf.reshape(b, s, d)
```

```python
import functools
import math

import jax
import jax.numpy as jnp
from jax import lax
from jax.experimental import pallas as pl
from jax.experimental.pallas import tpu as pltpu

EPS = 1e-6
NEG_BIG = -1e30
LOG2E = math.log2(math.e)
CHUNK = 64
HEAD_LANES = 128
HALF = 64
QK_SCALE = 1.0 / math.sqrt(HALF)
ATT_BLOCK = 256
DA_STREAMS = 4
SB_STREAMS = 8
SB_UNIT_ROWS = 256
SB_MAX_LOG2 = 126.0
ROW_TILE = 512
FFN_ROW_TILE = 1024
FF_CHUNK = 256
QKV_CHUNK = 512
VMEM_LIMIT = 56 * 1024 * 1024

F32 = jnp.float32
BF16 = jnp.bfloat16


def _rms(x, g):
    ms = jnp.mean(x * x, axis=-1, keepdims=True)
    return x * lax.rsqrt(ms + EPS) * g


def _dot(a, b):
    return lax.dot_general(a, b, (((1,), (0,)), ((), ())), preferred_element_type=F32)


def _dot_nt(a, b):
    return lax.dot_general(a, b, (((1,), (1,)), ((), ())), preferred_element_type=F32)


def _resident(shape):
    zeros = (0,) * len(shape)
    return pl.BlockSpec(shape, lambda *_: zeros, pipeline_mode=pl.Buffered(1))


def _params(n_axes):
    return pltpu.CompilerParams(
        dimension_semantics=("arbitrary",) * n_axes, vmem_limit_bytes=VMEM_LIMIT)


def _swiglu_half_step(read_x, write_x, gpre, gpost, win_ref, wout_ref, h_ref, acc_ref,
                      before=None, after=None):
    half = h_ref.shape[0] // 2
    halves = [slice(0, half), slice(half, 2 * half)]
    f = wout_ref.shape[0]

    def pre(rows):
        if before is not None:
            before(rows)
        h_ref[rows, :] = _rms(read_x(rows), gpre).astype(BF16)
        acc_ref[rows, :] = jnp.zeros((half, acc_ref.shape[1]), F32)

    def post(rows):
        write_x(rows, read_x(rows) + 0.5 * _rms(acc_ref[rows, :], gpost))
        if after is not None:
            after(rows)

    def chunks(rows, overlapped):
        for j in range(f // FF_CHUNK):
            lo = j * FF_CHUNK
            h = h_ref[rows, :]
            a = _dot(h, win_ref[:, lo:lo + FF_CHUNK])
            b = _dot(h, win_ref[:, f + lo:f + lo + FF_CHUNK])
            g = (a * jax.nn.sigmoid(a) * b).astype(BF16)
            acc_ref[rows, :] += _dot(g, wout_ref[lo:lo + FF_CHUNK, :])
            if j == 0:
                overlapped()

    pre(halves[0])
    chunks(halves[0], lambda: pre(halves[1]))
    chunks(halves[1], lambda: post(halves[0]))
    post(halves[1])


def _ffn_kernel(x_ref, gpre_ref, gpost_ref, win_ref, wout_ref, o_ref, h_ref, acc_ref):
    def write(rows, value):
        o_ref[rows, :] = value

    _swiglu_half_step(lambda rows: x_ref[rows, :], write, gpre_ref[...], gpost_ref[...],
                      win_ref, wout_ref, h_ref, acc_ref)


def _ffn(x, g_pre, g_post, w_in, w_out):
    n, d = x.shape
    win = w_in.astype(BF16)
    wout = w_out.astype(BF16)
    row = pl.BlockSpec((FFN_ROW_TILE, d), lambda i: (i, 0))
    return pl.pallas_call(
        _ffn_kernel,
        grid=(n // FFN_ROW_TILE,),
        in_specs=[row, _resident((1, d)), _resident((1, d)),
                  _resident(win.shape), _resident(wout.shape)],
        out_specs=row,
        out_shape=jax.ShapeDtypeStruct((n, d), F32),
        scratch_shapes=[pltpu.VMEM((FFN_ROW_TILE, d), BF16), pltpu.VMEM((FFN_ROW_TILE, d), F32)],
        compiler_params=_params(1),
        name="ffn",
    )(x, g_pre.reshape(1, d), g_post.reshape(1, d), win, wout)


def _qkv_kernel(x_ref, g_ref, w_ref, o_ref):
    h = _rms(x_ref[...], g_ref[...]).astype(BF16)
    for c in range(w_ref.shape[1] // QKV_CHUNK):
        cols = slice(c * QKV_CHUNK, (c + 1) * QKV_CHUNK)
        o_ref[:, cols] = _dot(h, w_ref[:, cols]).astype(BF16)


def _qkv_proj(x, g, w_qkv):
    n, d = x.shape
    d3 = w_qkv.shape[1]
    return pl.pallas_call(
        _qkv_kernel,
        grid=(n // ROW_TILE,),
        in_specs=[pl.BlockSpec((ROW_TILE, d), lambda i: (i, 0)),
                  _resident((1, d)), _resident((d, d3))],
        out_specs=pl.BlockSpec((ROW_TILE, d3), lambda i: (i, 0)),
        out_shape=jax.ShapeDtypeStruct((n, d3), BF16),
        compiler_params=_params(1),
        name="qkv_proj",
    )(x, g.reshape(1, d), w_qkv.astype(BF16))


def _stack_halves(q):
    lane = lax.broadcasted_iota(jnp.int32, q.shape, 1)
    qs = q.astype(F32) * QK_SCALE
    lo = jnp.where(lane < HALF, qs, 0.0)
    hi = jnp.where(lane >= HALF, qs, 0.0)
    return jnp.concatenate([lo, hi], axis=0).astype(BF16)


def _attn_specs(d, seq, n_streams):
    width = n_streams * HEAD_LANES
    n_groups = d // width

    def spec(third):
        return pl.BlockSpec((1, seq, width), lambda b, h, *_: (b, 0, third * n_groups + h))
    return n_groups, spec


def _lanes(s):
    return slice(s * HEAD_LANES, (s + 1) * HEAD_LANES)


def _da_kernel(slope_ref, lamv_ref, subln_ref, q_ref, k_ref, v_ref, o_ref,
               q2_ref, vones_ref, s_ref, mx_ref, acc_ref, *, lambda_init):
    t = ATT_BLOCK
    n_blocks = q_ref.shape[1] // t
    streams = range(DA_STREAMS)
    slopes = [slope_ref[pl.program_id(1) * DA_STREAMS + s] for s in streams]

    for s in streams:
        vones_ref[s, :, :HEAD_LANES] = v_ref[0, :, _lanes(s)]
        vones_ref[s, :, HEAD_LANES:] = jnp.ones((v_ref.shape[1], HEAD_LANES), BF16)

    lv = lamv_ref[...]
    lam = (jnp.exp(jnp.sum(lv[0:1] * lv[1:2], axis=1, keepdims=True))
           - jnp.exp(jnp.sum(lv[2:3] * lv[3:4], axis=1, keepdims=True)) + lambda_init)
    subln = subln_ref[...]

    row = lax.broadcasted_iota(jnp.int32, (t, t), 0)
    col = lax.broadcasted_iota(jnp.int32, (t, t), 1)
    diag_shape = (row - jnp.abs(row - col)).astype(F32)
    allowed = col // CHUNK <= row // CHUNK
    diag_bias = [jnp.where(allowed, slopes[s] * diag_shape, NEG_BIG) for s in streams]
    col_f = lax.broadcasted_iota(jnp.int32, (1, t), 1).astype(F32)

    units = [(s, c) for s in streams for c in range(2)]
    n_lane_tiles = t // HEAD_LANES

    def lane_fold(x, op):
        out = x[:, :HEAD_LANES]
        for c in range(1, n_lane_tiles):
            out = op(out, x[:, c * HEAD_LANES:(c + 1) * HEAD_LANES])
        return out

    def key_rows(j):
        return pl.ds(pl.multiple_of(j * t, t), t)

    def score_pass(tiles, first):
        work = [(j, bias, u) for j, bias in tiles for u in range(len(units))]
        raw = {}

        def matmul(n):
            j, _, u = work[n]
            s, c = units[u]
            raw[n] = _dot_nt(q2_ref[s, c * t:(c + 1) * t, :], k_ref[0, key_rows(j), _lanes(s)])

        def finish(n):
            j, bias, u = work[n]
            sc = raw.pop(n) + bias[units[u][0]]
            s_ref[u, j] = sc
            tile_max = lane_fold(sc, jnp.maximum)
            mx_ref[u] = tile_max if first else jnp.maximum(mx_ref[u], tile_max)

        for step in range(len(work) + 1):
            if step < len(work):
                matmul(step)
            if step >= 1:
                finish(step - 1)

    def value_pass(tiles, first):
        for u, (s, c) in enumerate(units):
            m = jnp.concatenate([mx_ref[u]] * n_lane_tiles, axis=1)
            pv = None
            for j in tiles:
                p = jnp.exp(s_ref[u, j] - m)
                part = _dot(p.astype(BF16), vones_ref[s, key_rows(j), :])
                pv = part if pv is None else pv + part
            if first:
                acc_ref[u] = pv
            else:
                acc_ref[u] += pv

    def finish_block(i):
        rows = pl.ds(pl.multiple_of(i * t, t), t)
        for s in streams:
            pv0, pv1 = acc_ref[2 * s], acc_ref[2 * s + 1]
            o = (pv0[:, :HEAD_LANES] / pv0[:, HEAD_LANES:]
                 - lam * (pv1[:, :HEAD_LANES] / pv1[:, HEAD_LANES:]))
            o = _rms(o, subln) * (1.0 - lambda_init)
            o_ref[0, rows, _lanes(s)] = o.astype(BF16)

    def q_block(i, carry):
        rows = pl.ds(pl.multiple_of(i * t, t), t)
        for s in streams:
            q2_ref[s] = _stack_halves(q_ref[0, rows, _lanes(s)])

        def earlier(j):
            back = jnp.asarray((i - j) * t, F32)
            return j, [slopes[s] * (col_f - back) for s in streams]

        score_pass([(i, diag_bias)], True)

        def score_pair(n, c):
            score_pass([earlier(2 * n), earlier(2 * n + 1)], False)
            return c

        lax.fori_loop(0, i // 2, score_pair, 0)

        @pl.when(i % 2 == 1)
        def _():
            score_pass([earlier(i - 1)], False)

        for u in range(len(units)):
            row_max = jnp.max(mx_ref[u], axis=1, keepdims=True)
            mx_ref[u] = jnp.broadcast_to(row_max, (t, HEAD_LANES))

        value_pass([0], True)

        def value_pair(n, c):
            value_pass([2 * n + 1, 2 * n + 2], False)
            return c

        lax.fori_loop(0, i // 2, value_pair, 0)

        @pl.when(i % 2 == 1)
        def _():
            value_pass([i], False)

        finish_block(i)
        return carry

    lax.fori_loop(0, n_blocks, q_block, 0)


def _diff_attention(qkv, lam_vecs, subln_g, lambda_init):
    b, s, d3 = qkv.shape
    d = d3 // 3
    nh = d // HEAD_LANES
    n_groups, spec = _attn_specs(d, s, DA_STREAMS)
    slopes = jnp.asarray([2.0 ** (-8.0 * (h + 1) / nh) for h in range(nh)], F32)
    grid_spec = pltpu.PrefetchScalarGridSpec(
        num_scalar_prefetch=1,
        grid=(b, n_groups),
        in_specs=[pl.BlockSpec(lam_vecs.shape, lambda *_: (0, 0)),
                  pl.BlockSpec((1, HEAD_LANES), lambda *_: (0, 0)),
                  spec(0), spec(1), spec(2)],
        out_specs=spec(0),
        scratch_shapes=[
            pltpu.VMEM((DA_STREAMS, 2 * ATT_BLOCK, HEAD_LANES), BF16),
            pltpu.VMEM((DA_STREAMS, s, 2 * HEAD_LANES), BF16),
            pltpu.VMEM((2 * DA_STREAMS, s // ATT_BLOCK, ATT_BLOCK, ATT_BLOCK), F32),
            pltpu.VMEM((2 * DA_STREAMS, ATT_BLOCK, HEAD_LANES), F32),
            pltpu.VMEM((2 * DA_STREAMS, ATT_BLOCK, 2 * HEAD_LANES), F32)],
    )
    return pl.pallas_call(
        functools.partial(_da_kernel, lambda_init=lambda_init),
        grid_spec=grid_spec,
        out_shape=jax.ShapeDtypeStruct((b, s, d), BF16),
        compiler_params=_params(2),
        name="diff_attention",
    )(slopes, lam_vecs, subln_g.reshape(1, HEAD_LANES), qkv, qkv, qkv)


def _sb_kernel(q_ref, k_ref, v_ref, o_ref, q2_ref, tail_ref, acc_ref):
    t = ATT_BLOCK
    n_blocks = q_ref.shape[1] // t
    streams = range(SB_STREAMS)
    cr = SB_UNIT_ROWS
    per_stream = 2 * t // cr
    units = [(s, c) for s in streams for c in range(per_stream)]
    n_units = len(units)

    row = lax.broadcasted_iota(jnp.int32, (2 * t, t), 0) % t
    col = lax.broadcasted_iota(jnp.int32, (2 * t, t), 1)
    strict = col < row
    strict_u = [strict[c * cr:(c + 1) * cr] for c in range(per_stream)]
    kr = lax.broadcasted_iota(jnp.int32, (t, t), 0)
    kc = lax.broadcasted_iota(jnp.int32, (t, t), 1)
    later = (kr > kc).astype(BF16)
    lane = lax.broadcasted_iota(jnp.int32, (t, HEAD_LANES), 1)

    def update(tiles, first):
        work = [(j, u) for j in tiles for u in range(n_units)]
        ws, mids = {}, {}

        def keys(n):
            return pl.ds(pl.multiple_of(work[n][0] * t, t), t)

        def stage_scores(n):
            s, c = units[work[n][1]]
            w = _dot_nt(q2_ref[s, c * cr:(c + 1) * cr, :], k_ref[0, keys(n), _lanes(s)]) * LOG2E
            ws[n] = jnp.minimum(w, SB_MAX_LOG2)

        def stage_costs(n):
            w = ws.pop(n)
            cost = jnp.log2(1.0 + jnp.exp2(w))
            log_beta = w - cost
            if first:
                cost = jnp.where(strict_u[units[work[n][1]][1]], cost, 0.0)
            sums = _dot(cost.astype(BF16), later)
            total = jnp.broadcast_to(sums[:, 0:1] + cost[:, 0:1], (cr, HEAD_LANES))
            mids[n] = (log_beta - sums, total)

        def stage_values(n):
            u = work[n][1]
            s, c = units[u]
            pre, total = mids.pop(n)
            if first:
                a = jnp.where(strict_u[c], jnp.exp2(pre), 0.0)
                acc_ref[u] = _dot(a.astype(BF16), v_ref[0, keys(n), _lanes(s)])
                tail_ref[u] = total
            else:
                tail = tail_ref[u]
                a = jnp.exp2(pre - jnp.concatenate([tail] * (t // HEAD_LANES), axis=1))
                acc_ref[u] += _dot(a.astype(BF16), v_ref[0, keys(n), _lanes(s)])
                tail_ref[u] = tail + total

        for step in range(len(work) + 2):
            if step < len(work):
                stage_scores(step)
            if 0 <= step - 1 < len(work):
                stage_costs(step - 1)
            if 0 <= step - 2 < len(work):
                stage_values(step - 2)

    def q_block(i, carry):
        rows = pl.ds(pl.multiple_of(i * t, t), t)
        for s in streams:
            q2_ref[s] = _stack_halves(q_ref[0, rows, _lanes(s)])
        update([i], True)

        def k_block(n, c):
            update([i - 1 - n], False)
            return c

        lax.fori_loop(0, i, k_block, 0)
        for s in streams:
            acc = jnp.concatenate([acc_ref[s * per_stream + c] for c in range(per_stream)], axis=0)
            o_ref[0, rows, _lanes(s)] = jnp.where(lane < HALF, acc[:t], acc[t:]).astype(BF16)
        return carry

    lax.fori_loop(0, n_blocks, q_block, 0)


def _stick_breaking_attention(qkv):
    b, s, d3 = qkv.shape
    d = d3 // 3
    n_groups, spec = _attn_specs(d, s, SB_STREAMS)
    n_units = SB_STREAMS * 2 * ATT_BLOCK // SB_UNIT_ROWS
    return pl.pallas_call(
        _sb_kernel,
        grid=(b, n_groups),
        in_specs=[spec(0), spec(1), spec(2)],
        out_specs=spec(0),
        out_shape=jax.ShapeDtypeStruct((b, s, d), BF16),
        scratch_shapes=[pltpu.VMEM((SB_STREAMS, 2 * ATT_BLOCK, HEAD_LANES), BF16),
                        pltpu.VMEM((n_units, SB_UNIT_ROWS, HEAD_LANES), F32),
                        pltpu.VMEM((n_units, SB_UNIT_ROWS, HEAD_LANES), F32)],
        compiler_params=_params(2),
        name="stick_breaking_attention",
    )(qkv, qkv, qkv)


def _post_mixer_kernel(x_ref, a_ref, p_ref, g_ref, wo_ref, win_ref, wout_ref, wproj_ref,
                       wgate_ref, o_ref, h_ref, acc_ref):
    def write(rows, value):
        o_ref[rows, :] = value

    def mixer_projection(rows):
        o_ref[rows, :] = x_ref[rows, :] + _rms(_dot(a_ref[rows, :], wo_ref[...]), g_ref[0:1])

    def layer_embedding(rows):
        x = o_ref[rows, :]
        e = _dot(p_ref[rows, :].astype(BF16), wproj_ref[...])
        gate = jax.nn.sigmoid(_dot(_rms(x, g_ref[3:4]).astype(BF16), wgate_ref[...]))
        o_ref[rows, :] = x + _rms(gate * e, g_ref[4:5])

    _swiglu_half_step(lambda rows: o_ref[rows, :], write, g_ref[1:2], g_ref[2:3],
                      win_ref, wout_ref, h_ref, acc_ref,
                      before=mixer_projection, after=layer_embedding)


def _post_mixer(x, attn, p, gains, w_o, w_in, w_out, w_proj, w_gate):
    n, d = x.shape
    dp = p.shape[1]
    row = pl.BlockSpec((FFN_ROW_TILE, d), lambda i: (i, 0))
    weights = [w.astype(BF16) for w in (w_o, w_in, w_out, w_proj, w_gate)]
    return pl.pallas_call(
        _post_mixer_kernel,
        grid=(n // FFN_ROW_TILE,),
        in_specs=[row, row, pl.BlockSpec((FFN_ROW_TILE, dp), lambda i: (i, 0)),
                  _resident(gains.shape)] + [_resident(w.shape) for w in weights],
        out_specs=row,
        out_shape=jax.ShapeDtypeStruct((n, d), F32),
        scratch_shapes=[pltpu.VMEM((FFN_ROW_TILE, d), BF16), pltpu.VMEM((FFN_ROW_TILE, d), F32)],
        compiler_params=_params(1),
        name="post_mixer",
    )(x, attn, p, gains, *weights)


def kernel(x, p, norm_gains, ffn1_w_in, ffn1_w_out, ffn2_w_in, ffn2_w_out, da_w_qkv, da_w_o,
           da_lambda, da_subln, sb_w_qkv, sb_w_o, ple_w_proj, ple_w_gate):
    b, s, d = x.shape
    depth = p.shape[0]
    n = b * s
    xf = x.reshape(n, d)
    for i in range(depth):
        g = norm_gains[i]
        j = i // 2
        xf = _ffn(xf, g[0], g[1], ffn1_w_in[i], ffn1_w_out[i])
        if i % 2 == 0:
            qkv = _qkv_proj(xf, g[2], da_w_qkv[j]).reshape(b, s, 3 * d)
            lambda_init = 0.8 - 0.6 * math.exp(-0.3 * i)
            attn = _diff_attention(qkv, da_lambda[j], da_subln[j], lambda_init)
            w_o = da_w_o[j]
        else:
            qkv = _qkv_proj(xf, g[2], sb_w_qkv[j]).reshape(b, s, 3 * d)
            attn = _stick_breaking_attention(qkv)
            w_o = sb_w_o[j]
        xf = _post_mixer(xf, attn.reshape(n, d), p[i].reshape(n, p.shape[-1]), g[3:8], w_o,
                         ffn2_w_in[i], ffn2_w_out[i], ple_w_proj[i], ple_w_gate[i])
    return xf.reshape(b, s, d)
```

```python
import functools
import math

import jax
import jax.numpy as jnp
from jax import lax
from jax.experimental import pallas as pl
from jax.experimental.pallas import tpu as pltpu

EPS = 1e-6
NEG_BIG = -1e30
LOG2E = math.log2(math.e)
CHUNK = 64
HEAD_LANES = 128
HALF = 64
QK_SCALE = 1.0 / math.sqrt(HALF)
ATT_BLOCK = 256
DA_STREAMS = 4
SB_STREAMS = 8
SB_UNIT_ROWS = 256
SB_MAX_LOG2 = 126.0
ROW_TILE = 512
FFN_ROW_TILE = 1024
FF_CHUNK = 256
QKV_CHUNK = 512
VMEM_LIMIT = 56 * 1024 * 1024

F32 = jnp.float32
BF16 = jnp.bfloat16


def _rms(x, g):
    ms = jnp.mean(x * x, axis=-1, keepdims=True)
    return x * lax.rsqrt(ms + EPS) * g


def _dot(a, b):
    return lax.dot_general(a, b, (((1,), (0,)), ((), ())), preferred_element_type=F32)


def _dot_nt(a, b):
    return lax.dot_general(a, b, (((1,), (1,)), ((), ())), preferred_element_type=F32)


def _resident(shape):
    zeros = (0,) * len(shape)
    return pl.BlockSpec(shape, lambda *_: zeros, pipeline_mode=pl.Buffered(1))


def _params(n_axes):
    return pltpu.CompilerParams(
        dimension_semantics=("arbitrary",) * n_axes, vmem_limit_bytes=VMEM_LIMIT)


def _swiglu_half_step(read_x, write_x, gpre, gpost, win_ref, wout_ref, h_ref, acc_ref,
                      before=None, after=None):
    half = h_ref.shape[0] // 2
    halves = [slice(0, half), slice(half, 2 * half)]
    f = wout_ref.shape[0]

    def pre(rows):
        if before is not None:
            before(rows)
        h_ref[rows, :] = _rms(read_x(rows), gpre).astype(BF16)
        acc_ref[rows, :] = jnp.zeros((half, acc_ref.shape[1]), F32)

    def post(rows):
        write_x(rows, read_x(rows) + 0.5 * _rms(acc_ref[rows, :], gpost))
        if after is not None:
            after(rows)

    def chunks(rows, overlapped):
        for j in range(f // FF_CHUNK):
            lo = j * FF_CHUNK
            h = h_ref[rows, :]
            a = _dot(h, win_ref[:, lo:lo + FF_CHUNK])
            b = _dot(h, win_ref[:, f + lo:f + lo + FF_CHUNK])
            g = (a * jax.nn.sigmoid(a) * b).astype(BF16)
            acc_ref[rows, :] += _dot(g, wout_ref[lo:lo + FF_CHUNK, :])
            if j == 0:
                overlapped()

    pre(halves[0])
    chunks(halves[0], lambda: pre(halves[1]))
    chunks(halves[1], lambda: post(halves[0]))
    post(halves[1])


def _ffn_kernel(x_ref, gpre_ref, gpost_ref, win_ref, wout_ref, o_ref, h_ref, acc_ref):
    def write(rows, value):
        o_ref[rows, :] = value

    _swiglu_half_step(lambda rows: x_ref[rows, :], write, gpre_ref[...], gpost_ref[...],
                      win_ref, wout_ref, h_ref, acc_ref)


def _ffn(x, g_pre, g_post, w_in, w_out):
    n, d = x.shape
    win = w_in.astype(BF16)
    wout = w_out.astype(BF16)
    row = pl.BlockSpec((FFN_ROW_TILE, d), lambda i: (i, 0))
    return pl.pallas_call(
        _ffn_kernel,
        grid=(n // FFN_ROW_TILE,),
        in_specs=[row, _resident((1, d)), _resident((1, d)),
                  _resident(win.shape), _resident(wout.shape)],
        out_specs=row,
        out_shape=jax.ShapeDtypeStruct((n, d), F32),
        scratch_shapes=[pltpu.VMEM((FFN_ROW_TILE, d), BF16), pltpu.VMEM((FFN_ROW_TILE, d), F32)],
        compiler_params=_params(1),
        name="ffn",
    )(x, g_pre.reshape(1, d), g_post.reshape(1, d), win, wout)


def _qkv_kernel(x_ref, g_ref, w_ref, o_ref):
    h = _rms(x_ref[...], g_ref[...]).astype(BF16)
    for c in range(w_ref.shape[1] // QKV_CHUNK):
        cols = slice(c * QKV_CHUNK, (c + 1) * QKV_CHUNK)
        o_ref[:, cols] = _dot(h, w_ref[:, cols]).astype(BF16)


def _qkv_proj(x, g, w_qkv):
    n, d = x.shape
    d3 = w_qkv.shape[1]
    return pl.pallas_call(
        _qkv_kernel,
        grid=(n // ROW_TILE,),
        in_specs=[pl.BlockSpec((ROW_TILE, d), lambda i: (i, 0)),
                  _resident((1, d)), _resident((d, d3))],
        out_specs=pl.BlockSpec((ROW_TILE, d3), lambda i: (i, 0)),
        out_shape=jax.ShapeDtypeStruct((n, d3), BF16),
        compiler_params=_params(1),
        name="qkv_proj",
    )(x, g.reshape(1, d), w_qkv.astype(BF16))


def _stack_halves(q):
    lane = lax.broadcasted_iota(jnp.int32, q.shape, 1)
    qs = q.astype(F32) * QK_SCALE
    lo = jnp.where(lane < HALF, qs, 0.0)
    hi = jnp.where(lane >= HALF, qs, 0.0)
    return jnp.concatenate([lo, hi], axis=0).astype(BF16)


def _attn_specs(d, seq, n_streams):
    width = n_streams * HEAD_LANES
    n_groups = d // width

    def spec(third):
        return pl.BlockSpec((1, seq, width), lambda b, h, *_: (b, 0, third * n_groups + h))
    return n_groups, spec


def _lanes(s):
    return slice(s * HEAD_LANES, (s + 1) * HEAD_LANES)


def _da_kernel(slope_ref, lamv_ref, subln_ref, q_ref, k_ref, v_ref, o_ref,
               q2_ref, vones_ref, s_ref, mx_ref, acc_ref, *, lambda_init):
    t = ATT_BLOCK
    n_blocks = q_ref.shape[1] // t
    streams = range(DA_STREAMS)
    slopes = [slope_ref[pl.program_id(1) * DA_STREAMS + s] for s in streams]

    for s in streams:
        vones_ref[s, :, :HEAD_LANES] = v_ref[0, :, _lanes(s)]
        vones_ref[s, :, HEAD_LANES:] = jnp.ones((v_ref.shape[1], HEAD_LANES), BF16)

    lv = lamv_ref[...]
    lam = (jnp.exp(jnp.sum(lv[0:1] * lv[1:2], axis=1, keepdims=True))
           - jnp.exp(jnp.sum(lv[2:3] * lv[3:4], axis=1, keepdims=True)) + lambda_init)
    subln = subln_ref[...]

    row = lax.broadcasted_iota(jnp.int32, (t, t), 0)
    col = lax.broadcasted_iota(jnp.int32, (t, t), 1)
    diag_shape = (row - jnp.abs(row - col)).astype(F32)
    allowed = col // CHUNK <= row // CHUNK
    diag_bias = [jnp.where(allowed, slopes[s] * diag_shape, NEG_BIG) for s in streams]
    col_f = lax.broadcasted_iota(jnp.int32, (1, t), 1).astype(F32)

    units = [(s, c) for s in streams for c in range(2)]
    n_lane_tiles = t // HEAD_LANES

    def lane_fold(x, op):
        out = x[:, :HEAD_LANES]
        for c in range(1, n_lane_tiles):
            out = op(out, x[:, c * HEAD_LANES:(c + 1) * HEAD_LANES])
        return out

    def key_rows(j):
        return pl.ds(pl.multiple_of(j * t, t), t)

    def score_pass(tiles, first):
        work = [(j, bias, u) for j, bias in tiles for u in range(len(units))]
        raw = {}

        def matmul(n):
            j, _, u = work[n]
            s, c = units[u]
            raw[n] = _dot_nt(q2_ref[s, c * t:(c + 1) * t, :], k_ref[0, key_rows(j), _lanes(s)])

        def finish(n):
            j, bias, u = work[n]
            sc = raw.pop(n) + bias[units[u][0]]
            s_ref[u, j] = sc
            tile_max = lane_fold(sc, jnp.maximum)
            mx_ref[u] = tile_max if first else jnp.maximum(mx_ref[u], tile_max)

        for step in range(len(work) + 1):
            if step < len(work):
                matmul(step)
            if step >= 1:
                finish(step - 1)

    def value_pass(tiles, first):
        for u, (s, c) in enumerate(units):
            m = jnp.concatenate([mx_ref[u]] * n_lane_tiles, axis=1)
            pv = None
            for j in tiles:
                p = jnp.exp(s_ref[u, j] - m)
                part = _dot(p.astype(BF16), vones_ref[s, key_rows(j), :])
                pv = part if pv is None else pv + part
            if first:
                acc_ref[u] = pv
            else:
                acc_ref[u] += pv

    def finish_block(i):
        rows = pl.ds(pl.multiple_of(i * t, t), t)
        for s in streams:
            pv0, pv1 = acc_ref[2 * s], acc_ref[2 * s + 1]
            o = (pv0[:, :HEAD_LANES] / pv0[:, HEAD_LANES:]
                 - lam * (pv1[:, :HEAD_LANES] / pv1[:, HEAD_LANES:]))
            o = _rms(o, subln) * (1.0 - lambda_init)
            o_ref[0, rows, _lanes(s)] = o.astype(BF16)

    def q_block(i, carry):
        rows = pl.ds(pl.multiple_of(i * t, t), t)
        for s in streams:
            q2_ref[s] = _stack_halves(q_ref[0, rows, _lanes(s)])

        def earlier(j):
            back = jnp.asarray((i - j) * t, F32)
            return j, [slopes[s] * (col_f - back) for s in streams]

        score_pass([(i, diag_bias)], True)

        def score_pair(n, c):
            score_pass([earlier(2 * n), earlier(2 * n + 1)], False)
            return c

        lax.fori_loop(0, i // 2, score_pair, 0)

        @pl.when(i % 2 == 1)
        def _():
            score_pass([earlier(i - 1)], False)

        for u in range(len(units)):
            row_max = jnp.max(mx_ref[u], axis=1, keepdims=True)
            mx_ref[u] = jnp.broadcast_to(row_max, (t, HEAD_LANES))

        value_pass([0], True)

        def value_pair(n, c):
            value_pass([2 * n + 1, 2 * n + 2], False)
            return c

        lax.fori_loop(0, i // 2, value_pair, 0)

        @pl.when(i % 2 == 1)
        def _():
            value_pass([i], False)

        finish_block(i)
        return carry

    lax.fori_loop(0, n_blocks, q_block, 0)


def _diff_attention(qkv, lam_vecs, subln_g, lambda_init):
    b, s, d3 = qkv.shape
    d = d3 // 3
    nh = d // HEAD_LANES
    n_groups, spec = _attn_specs(d, s, DA_STREAMS)
    slopes = jnp.asarray([2.0 ** (-8.0 * (h + 1) / nh) for h in range(nh)], F32)
    grid_spec = pltpu.PrefetchScalarGridSpec(
        num_scalar_prefetch=1,
        grid=(b, n_groups),
        in_specs=[pl.BlockSpec(lam_vecs.shape, lambda *_: (0, 0)),
                  pl.BlockSpec((1, HEAD_LANES), lambda *_: (0, 0)),
                  spec(0), spec(1), spec(2)],
        out_specs=spec(0),
        scratch_shapes=[
            pltpu.VMEM((DA_STREAMS, 2 * ATT_BLOCK, HEAD_LANES), BF16),
            pltpu.VMEM((DA_STREAMS, s, 2 * HEAD_LANES), BF16),
            pltpu.VMEM((2 * DA_STREAMS, s // ATT_BLOCK, ATT_BLOCK, ATT_BLOCK), F32),
            pltpu.VMEM((2 * DA_STREAMS, ATT_BLOCK, HEAD_LANES), F32),
            pltpu.VMEM((2 * DA_STREAMS, ATT_BLOCK, 2 * HEAD_LANES), F32)],
    )
    return pl.pallas_call(
        functools.partial(_da_kernel, lambda_init=lambda_init),
        grid_spec=grid_spec,
        out_shape=jax.ShapeDtypeStruct((b, s, d), BF16),
        compiler_params=_params(2),
        name="diff_attention",
    )(slopes, lam_vecs, subln_g.reshape(1, HEAD_LANES), qkv, qkv, qkv)


def _sb_kernel(q_ref, k_ref, v_ref, o_ref, q2_ref, tail_ref, acc_ref):
    t = ATT_BLOCK
    n_blocks = q_ref.shape[1] // t
    streams = range(SB_STREAMS)
    cr = SB_UNIT_ROWS
    per_stream = 2 * t // cr
    units = [(s, c) for s in streams for c in range(per_stream)]
    n_units = len(units)

    row = lax.broadcasted_iota(jnp.int32, (2 * t, t), 0) % t
    col = lax.broadcasted_iota(jnp.int32, (2 * t, t), 1)
    strict = col < row
    strict_u = [strict[c * cr:(c + 1) * cr] for c in range(per_stream)]
    kr = lax.broadcasted_iota(jnp.int32, (t, t), 0)
    kc = lax.broadcasted_iota(jnp.int32, (t, t), 1)
    later = (kr > kc).astype(BF16)
    lane = lax.broadcasted_iota(jnp.int32, (t, HEAD_LANES), 1)

    def update(tiles, first):
        work = [(j, u) for j in tiles for u in range(n_units)]
        ws, mids = {}, {}

        def keys(n):
            return pl.ds(pl.multiple_of(work[n][0] * t, t), t)

        def stage_scores(n):
            s, c = units[work[n][1]]
            w = _dot_nt(q2_ref[s, c * cr:(c + 1) * cr, :], k_ref[0, keys(n), _lanes(s)]) * LOG2E
            ws[n] = jnp.minimum(w, SB_MAX_LOG2)

        def stage_costs(n):
            w = ws.pop(n)
            cost = jnp.log2(1.0 + jnp.exp2(w))
            log_beta = w - cost
            if first:
                cost = jnp.where(strict_u[units[work[n][1]][1]], cost, 0.0)
            sums = _dot(cost, later)
            total = jnp.broadcast_to(sums[:, 0:1] + cost[:, 0:1], (cr, HEAD_LANES))
            mids[n] = (log_beta - sums, total)

        def stage_values(n):
            u = work[n][1]
            s, c = units[u]
            pre, total = mids.pop(n)
            if first:
                a = jnp.where(strict_u[c], jnp.exp2(pre), 0.0)
                acc_ref[u] = _dot(a, v_ref[0, keys(n), _lanes(s)])
                tail_ref[u] = total
            else:
                tail = tail_ref[u]
                a = jnp.exp2(pre - jnp.concatenate([tail] * (t // HEAD_LANES), axis=1))
                acc_ref[u] += _dot(a, v_ref[0, keys(n), _lanes(s)])
                tail_ref[u] = tail + total

        for step in range(len(work) + 2):
            if step < len(work):
                stage_scores(step)
            if 0 <= step - 1 < len(work):
                stage_costs(step - 1)
            if 0 <= step - 2 < len(work):
                stage_values(step - 2)

    def q_block(i, carry):
        rows = pl.ds(pl.multiple_of(i * t, t), t)
        for s in streams:
            q2_ref[s] = _stack_halves(q_ref[0, rows, _lanes(s)])
        update([i], True)

        def k_block(n, c):
            update([i - 1 - n], False)
            return c

        lax.fori_loop(0, i, k_block, 0)
        for s in streams:
            acc = jnp.concatenate([acc_ref[s * per_stream + c] for c in range(per_stream)], axis=0)
            o_ref[0, rows, _lanes(s)] = jnp.where(lane < HALF, acc[:t], acc[t:]).astype(BF16)
        return carry

    lax.fori_loop(0, n_blocks, q_block, 0)


def _stick_breaking_attention(qkv):
    b, s, d3 = qkv.shape
    d = d3 // 3
    n_groups, spec = _attn_specs(d, s, SB_STREAMS)
    n_units = SB_STREAMS * 2 * ATT_BLOCK // SB_UNIT_ROWS
    return pl.pallas_call(
        _sb_kernel,
        grid=(b, n_groups),
        in_specs=[spec(0), spec(1), spec(2)],
        out_specs=spec(0),
        out_shape=jax.ShapeDtypeStruct((b, s, d), BF16),
        scratch_shapes=[pltpu.VMEM((SB_STREAMS, 2 * ATT_BLOCK, HEAD_LANES), BF16),
                        pltpu.VMEM((n_units, SB_UNIT_ROWS, HEAD_LANES), F32),
                        pltpu.VMEM((n_units, SB_UNIT_ROWS, HEAD_LANES), F32)],
        compiler_params=_params(2),
        name="stick_breaking_attention",
    )(qkv, qkv, qkv)


def _post_mixer_kernel(x_ref, a_ref, p_ref, g_ref, wo_ref, win_ref, wout_ref, wproj_ref,
                       wgate_ref, o_ref, h_ref, acc_ref):
    def write(rows, value):
        o_ref[rows, :] = value

    def mixer_projection(rows):
        o_ref[rows, :] = x_ref[rows, :] + _rms(_dot(a_ref[rows, :], wo_ref[...]), g_ref[0:1])

    def layer_embedding(rows):
        x = o_ref[rows, :]
        e = _dot(p_ref[rows, :].astype(BF16), wproj_ref[...])
        gate = jax.nn.sigmoid(_dot(_rms(x, g_ref[3:4]).astype(BF16), wgate_ref[...]))
        o_ref[rows, :] = x + _rms(gate * e, g_ref[4:5])

    _swiglu_half_step(lambda rows: o_ref[rows, :], write, g_ref[1:2], g_ref[2:3],
                      win_ref, wout_ref, h_ref, acc_ref,
                      before=mixer_projection, after=layer_embedding)


def _post_mixer(x, attn, p, gains, w_o, w_in, w_out, w_proj, w_gate):
    n, d = x.shape
    dp = p.shape[1]
    row = pl.BlockSpec((FFN_ROW_TILE, d), lambda i: (i, 0))
    weights = [w.astype(BF16) for w in (w_o, w_in, w_out, w_proj, w_gate)]
    return pl.pallas_call(
        _post_mixer_kernel,
        grid=(n // FFN_ROW_TILE,),
        in_specs=[row, row, pl.BlockSpec((FFN_ROW_TILE, dp), lambda i: (i, 0)),
                  _resident(gains.shape)] + [_resident(w.shape) for w in weights],
        out_specs=row,
        out_shape=jax.ShapeDtypeStruct((n, d), F32),
        scratch_shapes=[pltpu.VMEM((FFN_ROW_TILE, d), BF16), pltpu.VMEM((FFN_ROW_TILE, d), F32)],
        compiler_params=_params(1),
        name="post_mixer",
    )(x, attn, p, gains, *weights)


def kernel(x, p, norm_gains, ffn1_w_in, ffn1_w_out, ffn2_w_in, ffn2_w_out, da_w_qkv, da_w_o,
           da_lambda, da_subln, sb_w_qkv, sb_w_o, ple_w_proj, ple_w_gate):
    b, s, d = x.shape
    depth = p.shape[0]
    n = b * s
    xf = x.reshape(n, d)
    for i in range(depth):
        g = norm_gains[i]
        j = i // 2
        xf = _ffn(xf, g[0], g[1], ffn1_w_in[i], ffn1_w_out[i])
        if i % 2 == 0:
            qkv = _qkv_proj(xf, g[2], da_w_qkv[j]).reshape(b, s, 3 * d)
            lambda_init = 0.8 - 0.6 * math.exp(-0.3 * i)
            attn = _diff_attention(qkv, da_lambda[j], da_subln[j], lambda_init)
            w_o = da_w_o[j]
        else:
            qkv = _qkv_proj(xf, g[2], sb_w_qkv[j]).reshape(b, s, 3 * d)
            attn = _stick_breaking_attention(qkv)
            w_o = sb_w_o[j]
        xf = _post_mixer(xf, attn.reshape(n, d), p[i].reshape(n, p.shape[-1]), g[3:8], w_o,
                         ffn2_w_in[i], ffn2_w_out[i], ple_w_proj[i], ple_w_gate[i])
    return xf.reshape(b, s, d)
```

```python
import functools
import math

import jax
import jax.numpy as jnp
from jax import lax
from jax.experimental import pallas as pl
from jax.experimental.pallas import tpu as pltpu

EPS = 1e-6
NEG_BIG = -1e30
LOG2E = math.log2(math.e)
CHUNK = 64
HEAD_LANES = 128
HALF = 64
QK_SCALE = 1.0 / math.sqrt(HALF)
ATT_BLOCK = 256
DA_STREAMS = 4
SB_STREAMS = 8
SB_UNIT_ROWS = 256
SB_MAX_LOG2 = 126.0
ROW_TILE = 512
FFN_ROW_TILE = 1024
FF_CHUNK = 256
QKV_CHUNK = 512
VMEM_LIMIT = 56 * 1024 * 1024

F32 = jnp.float32
BF16 = jnp.bfloat16


def _rms(x, g):
    ms = jnp.mean(x * x, axis=-1, keepdims=True)
    return x * lax.rsqrt(ms + EPS) * g


def _dot(a, b):
    return lax.dot_general(a, b, (((1,), (0,)), ((), ())), preferred_element_type=F32)


def _dot_nt(a, b):
    return lax.dot_general(a, b, (((1,), (1,)), ((), ())), preferred_element_type=F32)


def _resident(shape):
    zeros = (0,) * len(shape)
    return pl.BlockSpec(shape, lambda *_: zeros, pipeline_mode=pl.Buffered(1))


def _params(n_axes):
    return pltpu.CompilerParams(
        dimension_semantics=("arbitrary",) * n_axes, vmem_limit_bytes=VMEM_LIMIT)


def _swiglu_half_step(read_x, write_x, gpre, gpost, win_ref, wout_ref, h_ref, acc_ref,
                      before=None, after=None):
    half = h_ref.shape[0] // 2
    halves = [slice(0, half), slice(half, 2 * half)]
    f = wout_ref.shape[0]

    def pre(rows):
        if before is not None:
            before(rows)
        h_ref[rows, :] = _rms(read_x(rows), gpre).astype(BF16)
        acc_ref[rows, :] = jnp.zeros((half, acc_ref.shape[1]), F32)

    def post(rows):
        write_x(rows, read_x(rows) + 0.5 * _rms(acc_ref[rows, :], gpost))
        if after is not None:
            after(rows)

    def chunks(rows, overlapped):
        for j in range(f // FF_CHUNK):
            lo = j * FF_CHUNK
            h = h_ref[rows, :]
            a = _dot(h, win_ref[:, lo:lo + FF_CHUNK])
            b = _dot(h, win_ref[:, f + lo:f + lo + FF_CHUNK])
            g = (a * jax.nn.sigmoid(a) * b).astype(BF16)
            acc_ref[rows, :] += _dot(g, wout_ref[lo:lo + FF_CHUNK, :])
            if j == 0:
                overlapped()

    pre(halves[0])
    chunks(halves[0], lambda: pre(halves[1]))
    chunks(halves[1], lambda: post(halves[0]))
    post(halves[1])


def _ffn_kernel(x_ref, gpre_ref, gpost_ref, win_ref, wout_ref, o_ref, h_ref, acc_ref):
    def write(rows, value):
        o_ref[rows, :] = value

    _swiglu_half_step(lambda rows: x_ref[rows, :], write, gpre_ref[...], gpost_ref[...],
                      win_ref, wout_ref, h_ref, acc_ref)


def _ffn(x, g_pre, g_post, w_in, w_out):
    n, d = x.shape
    win = w_in.astype(BF16)
    wout = w_out.astype(BF16)
    row = pl.BlockSpec((FFN_ROW_TILE, d), lambda i: (i, 0))
    return pl.pallas_call(
        _ffn_kernel,
        grid=(n // FFN_ROW_TILE,),
        in_specs=[row, _resident((1, d)), _resident((1, d)),
                  _resident(win.shape), _resident(wout.shape)],
        out_specs=row,
        out_shape=jax.ShapeDtypeStruct((n, d), F32),
        scratch_shapes=[pltpu.VMEM((FFN_ROW_TILE, d), BF16), pltpu.VMEM((FFN_ROW_TILE, d), F32)],
        compiler_params=_params(1),
        name="ffn",
    )(x, g_pre.reshape(1, d), g_post.reshape(1, d), win, wout)


def _qkv_kernel(x_ref, g_ref, w_ref, o_ref):
    h = _rms(x_ref[...], g_ref[...]).astype(BF16)
    for c in range(w_ref.shape[1] // QKV_CHUNK):
        cols = slice(c * QKV_CHUNK, (c + 1) * QKV_CHUNK)
        o_ref[:, cols] = _dot(h, w_ref[:, cols]).astype(BF16)


def _qkv_proj(x, g, w_qkv):
    n, d = x.shape
    d3 = w_qkv.shape[1]
    return pl.pallas_call(
        _qkv_kernel,
        grid=(n // ROW_TILE,),
        in_specs=[pl.BlockSpec((ROW_TILE, d), lambda i: (i, 0)),
                  _resident((1, d)), _resident((d, d3))],
        out_specs=pl.BlockSpec((ROW_TILE, d3), lambda i: (i, 0)),
        out_shape=jax.ShapeDtypeStruct((n, d3), BF16),
        compiler_params=_params(1),
        name="qkv_proj",
    )(x, g.reshape(1, d), w_qkv.astype(BF16))


def _stack_halves(q):
    lane = lax.broadcasted_iota(jnp.int32, q.shape, 1)
    qs = q.astype(F32) * QK_SCALE
    lo = jnp.where(lane < HALF, qs, 0.0)
    hi = jnp.where(lane >= HALF, qs, 0.0)
    return jnp.concatenate([lo, hi], axis=0).astype(BF16)


def _attn_specs(d, seq, n_streams):
    width = n_streams * HEAD_LANES
    n_groups = d // width

    def spec(third):
        return pl.BlockSpec((1, seq, width), lambda b, h, *_: (b, 0, third * n_groups + h))
    return n_groups, spec


def _lanes(s):
    return slice(s * HEAD_LANES, (s + 1) * HEAD_LANES)


def _da_kernel(slope_ref, lamv_ref, subln_ref, q_ref, k_ref, v_ref, o_ref,
               q2_ref, vones_ref, s_ref, mx_ref, acc_ref, *, lambda_init):
    t = ATT_BLOCK
    n_blocks = q_ref.shape[1] // t
    streams = range(DA_STREAMS)
    slopes = [slope_ref[pl.program_id(1) * DA_STREAMS + s] for s in streams]

    for s in streams:
        vones_ref[s, :, :HEAD_LANES] = v_ref[0, :, _lanes(s)]
        vones_ref[s, :, HEAD_LANES:] = jnp.ones((v_ref.shape[1], HEAD_LANES), BF16)

    lv = lamv_ref[...]
    lam = (jnp.exp(jnp.sum(lv[0:1] * lv[1:2], axis=1, keepdims=True))
           - jnp.exp(jnp.sum(lv[2:3] * lv[3:4], axis=1, keepdims=True)) + lambda_init)
    subln = subln_ref[...]

    row = lax.broadcasted_iota(jnp.int32, (t, t), 0)
    col = lax.broadcasted_iota(jnp.int32, (t, t), 1)
    diag_shape = (row - jnp.abs(row - col)).astype(F32)
    allowed = col // CHUNK <= row // CHUNK
    diag_bias = [jnp.where(allowed, slopes[s] * diag_shape, NEG_BIG) for s in streams]
    col_f = lax.broadcasted_iota(jnp.int32, (1, t), 1).astype(F32)

    units = [(s, c) for s in streams for c in range(2)]
    n_lane_tiles = t // HEAD_LANES

    def lane_fold(x, op):
        out = x[:, :HEAD_LANES]
        for c in range(1, n_lane_tiles):
            out = op(out, x[:, c * HEAD_LANES:(c + 1) * HEAD_LANES])
        return out

    def key_rows(j):
        return pl.ds(pl.multiple_of(j * t, t), t)

    def score_pass(tiles, first):
        work = [(j, bias, u) for j, bias in tiles for u in range(len(units))]
        raw = {}

        def matmul(n):
            j, _, u = work[n]
            s, c = units[u]
            raw[n] = _dot_nt(q2_ref[s, c * t:(c + 1) * t, :], k_ref[0, key_rows(j), _lanes(s)])

        def finish(n):
            j, bias, u = work[n]
            sc = raw.pop(n) + bias[units[u][0]]
            s_ref[u, j] = sc
            tile_max = lane_fold(sc, jnp.maximum)
            mx_ref[u] = tile_max if first else jnp.maximum(mx_ref[u], tile_max)

        for step in range(len(work) + 1):
            if step < len(work):
                matmul(step)
            if step >= 1:
                finish(step - 1)

    def value_pass(tiles, first):
        for u, (s, c) in enumerate(units):
            m = jnp.concatenate([mx_ref[u]] * n_lane_tiles, axis=1)
            pv = None
            for j in tiles:
                p = jnp.exp(s_ref[u, j] - m)
                part = _dot(p, vones_ref[s, key_rows(j), :])
                pv = part if pv is None else pv + part
            if first:
                acc_ref[u] = pv
            else:
                acc_ref[u] += pv

    def finish_block(i):
        rows = pl.ds(pl.multiple_of(i * t, t), t)
        for s in streams:
            pv0, pv1 = acc_ref[2 * s], acc_ref[2 * s + 1]
            o = (pv0[:, :HEAD_LANES] / pv0[:, HEAD_LANES:]
                 - lam * (pv1[:, :HEAD_LANES] / pv1[:, HEAD_LANES:]))
            o = _rms(o, subln) * (1.0 - lambda_init)
            o_ref[0, rows, _lanes(s)] = o.astype(BF16)

    def q_block(i, carry):
        rows = pl.ds(pl.multiple_of(i * t, t), t)
        for s in streams:
            q2_ref[s] = _stack_halves(q_ref[0, rows, _lanes(s)])

        def earlier(j):
            back = jnp.asarray((i - j) * t, F32)
            return j, [slopes[s] * (col_f - back) for s in streams]

        score_pass([(i, diag_bias)], True)

        def score_pair(n, c):
            score_pass([earlier(2 * n), earlier(2 * n + 1)], False)
            return c

        lax.fori_loop(0, i // 2, score_pair, 0)

        @pl.when(i % 2 == 1)
        def _():
            score_pass([earlier(i - 1)], False)

        for u in range(len(units)):
            row_max = jnp.max(mx_ref[u], axis=1, keepdims=True)
            mx_ref[u] = jnp.broadcast_to(row_max, (t, HEAD_LANES))

        value_pass([0], True)

        def value_pair(n, c):
            value_pass([2 * n + 1, 2 * n + 2], False)
            return c

        lax.fori_loop(0, i // 2, value_pair, 0)

        @pl.when(i % 2 == 1)
        def _():
            value_pass([i], False)

        finish_block(i)
        return carry

    lax.fori_loop(0, n_blocks, q_block, 0)


def _diff_attention(qkv, lam_vecs, subln_g, lambda_init):
    b, s, d3 = qkv.shape
    d = d3 // 3
    nh = d // HEAD_LANES
    n_groups, spec = _attn_specs(d, s, DA_STREAMS)
    slopes = jnp.asarray([2.0 ** (-8.0 * (h + 1) / nh) for h in range(nh)], F32)
    grid_spec = pltpu.PrefetchScalarGridSpec(
        num_scalar_prefetch=1,
        grid=(b, n_groups),
        in_specs=[pl.BlockSpec(lam_vecs.shape, lambda *_: (0, 0)),
                  pl.BlockSpec((1, HEAD_LANES), lambda *_: (0, 0)),
                  spec(0), spec(1), spec(2)],
        out_specs=spec(0),
        scratch_shapes=[
            pltpu.VMEM((DA_STREAMS, 2 * ATT_BLOCK, HEAD_LANES), BF16),
            pltpu.VMEM((DA_STREAMS, s, 2 * HEAD_LANES), BF16),
            pltpu.VMEM((2 * DA_STREAMS, s // ATT_BLOCK, ATT_BLOCK, ATT_BLOCK), F32),
            pltpu.VMEM((2 * DA_STREAMS, ATT_BLOCK, HEAD_LANES), F32),
            pltpu.VMEM((2 * DA_STREAMS, ATT_BLOCK, 2 * HEAD_LANES), F32)],
    )
    return pl.pallas_call(
        functools.partial(_da_kernel, lambda_init=lambda_init),
        grid_spec=grid_spec,
        out_shape=jax.ShapeDtypeStruct((b, s, d), BF16),
        compiler_params=_params(2),
        name="diff_attention",
    )(slopes, lam_vecs, subln_g.reshape(1, HEAD_LANES), qkv, qkv, qkv)


def _sb_kernel(q_ref, k_ref, v_ref, o_ref, q2_ref, tail_ref, acc_ref):
    t = ATT_BLOCK
    n_blocks = q_ref.shape[1] // t
    streams = range(SB_STREAMS)
    cr = SB_UNIT_ROWS
    per_stream = 2 * t // cr
    units = [(s, c) for s in streams for c in range(per_stream)]
    n_units = len(units)

    row = lax.broadcasted_iota(jnp.int32, (2 * t, t), 0) % t
    col = lax.broadcasted_iota(jnp.int32, (2 * t, t), 1)
    strict = col < row
    strict_u = [strict[c * cr:(c + 1) * cr] for c in range(per_stream)]
    kr = lax.broadcasted_iota(jnp.int32, (t, t), 0)
    kc = lax.broadcasted_iota(jnp.int32, (t, t), 1)
    later = (kr > kc).astype(BF16)
    lane = lax.broadcasted_iota(jnp.int32, (t, HEAD_LANES), 1)

    def update(tiles, first):
        work = [(j, u) for j in tiles for u in range(n_units)]
        ws, mids = {}, {}

        def keys(n):
            return pl.ds(pl.multiple_of(work[n][0] * t, t), t)

        def stage_scores(n):
            s, c = units[work[n][1]]
            w = _dot_nt(q2_ref[s, c * cr:(c + 1) * cr, :], k_ref[0, keys(n), _lanes(s)]) * LOG2E
            ws[n] = jnp.minimum(w, SB_MAX_LOG2)

        def stage_costs(n):
            w = ws.pop(n)
            cost = jnp.log2(1.0 + jnp.exp2(w))
            log_beta = w - cost
            if first:
                cost = jnp.where(strict_u[units[work[n][1]][1]], cost, 0.0)
            sums = _dot(cost, later)
            total = jnp.broadcast_to(sums[:, 0:1] + cost[:, 0:1], (cr, HEAD_LANES))
            mids[n] = (log_beta - sums, total)

        def stage_values(n):
            u = work[n][1]
            s, c = units[u]
            pre, total = mids.pop(n)
            if first:
                a = jnp.where(strict_u[c], jnp.exp2(pre), 0.0)
                acc_ref[u] = _dot(a, v_ref[0, keys(n), _lanes(s)])
                tail_ref[u] = total
            else:
                tail = tail_ref[u]
                a = jnp.exp2(pre - jnp.concatenate([tail] * (t // HEAD_LANES), axis=1))
                acc_ref[u] += _dot(a, v_ref[0, keys(n), _lanes(s)])
                tail_ref[u] = tail + total

        for step in range(len(work) + 2):
            if step < len(work):
                stage_scores(step)
            if 0 <= step - 1 < len(work):
                stage_costs(step - 1)
            if 0 <= step - 2 < len(work):
                stage_values(step - 2)

    def q_block(i, carry):
        rows = pl.ds(pl.multiple_of(i * t, t), t)
        for s in streams:
            q2_ref[s] = _stack_halves(q_ref[0, rows, _lanes(s)])
        update([i], True)

        def k_pair(n, c):
            update([i - 1 - 2 * n, i - 2 - 2 * n], False)
            return c

        lax.fori_loop(0, i // 2, k_pair, 0)

        @pl.when(i % 2 == 1)
        def _():
            update([0], False)

        for s in streams:
            acc = jnp.concatenate([acc_ref[s * per_stream + c] for c in range(per_stream)], axis=0)
            o_ref[0, rows, _lanes(s)] = jnp.where(lane < HALF, acc[:t], acc[t:]).astype(BF16)
        return carry

    lax.fori_loop(0, n_blocks, q_block, 0)


def _stick_breaking_attention(qkv):
    b, s, d3 = qkv.shape
    d = d3 // 3
    n_groups, spec = _attn_specs(d, s, SB_STREAMS)
    n_units = SB_STREAMS * 2 * ATT_BLOCK // SB_UNIT_ROWS
    return pl.pallas_call(
        _sb_kernel,
        grid=(b, n_groups),
        in_specs=[spec(0), spec(1), spec(2)],
        out_specs=spec(0),
        out_shape=jax.ShapeDtypeStruct((b, s, d), BF16),
        scratch_shapes=[pltpu.VMEM((SB_STREAMS, 2 * ATT_BLOCK, HEAD_LANES), BF16),
                        pltpu.VMEM((n_units, SB_UNIT_ROWS, HEAD_LANES), F32),
                        pltpu.VMEM((n_units, SB_UNIT_ROWS, HEAD_LANES), F32)],
        compiler_params=_params(2),
        name="stick_breaking_attention",
    )(qkv, qkv, qkv)


def _post_mixer_kernel(x_ref, a_ref, p_ref, g_ref, wo_ref, win_ref, wout_ref, wproj_ref,
                       wgate_ref, o_ref, h_ref, acc_ref):
    def write(rows, value):
        o_ref[rows, :] = value

    def mixer_projection(rows):
        o_ref[rows, :] = x_ref[rows, :] + _rms(_dot(a_ref[rows, :], wo_ref[...]), g_ref[0:1])

    def layer_embedding(rows):
        x = o_ref[rows, :]
        e = _dot(p_ref[rows, :].astype(BF16), wproj_ref[...])
        gate = jax.nn.sigmoid(_dot(_rms(x, g_ref[3:4]).astype(BF16), wgate_ref[...]))
        o_ref[rows, :] = x + _rms(gate * e, g_ref[4:5])

    _swiglu_half_step(lambda rows: o_ref[rows, :], write, g_ref[1:2], g_ref[2:3],
                      win_ref, wout_ref, h_ref, acc_ref,
                      before=mixer_projection, after=layer_embedding)


def _post_mixer(x, attn, p, gains, w_o, w_in, w_out, w_proj, w_gate):
    n, d = x.shape
    dp = p.shape[1]
    row = pl.BlockSpec((FFN_ROW_TILE, d), lambda i: (i, 0))
    weights = [w.astype(BF16) for w in (w_o, w_in, w_out, w_proj, w_gate)]
    return pl.pallas_call(
        _post_mixer_kernel,
        grid=(n // FFN_ROW_TILE,),
        in_specs=[row, row, pl.BlockSpec((FFN_ROW_TILE, dp), lambda i: (i, 0)),
                  _resident(gains.shape)] + [_resident(w.shape) for w in weights],
        out_specs=row,
        out_shape=jax.ShapeDtypeStruct((n, d), F32),
        scratch_shapes=[pltpu.VMEM((FFN_ROW_TILE, d), BF16), pltpu.VMEM((FFN_ROW_TILE, d), F32)],
        compiler_params=_params(1),
        name="post_mixer",
    )(x, attn, p, gains, *weights)


def kernel(x, p, norm_gains, ffn1_w_in, ffn1_w_out, ffn2_w_in, ffn2_w_out, da_w_qkv, da_w_o,
           da_lambda, da_subln, sb_w_qkv, sb_w_o, ple_w_proj, ple_w_gate):
    b, s, d = x.shape
    depth = p.shape[0]
    n = b * s
    xf = x.reshape(n, d)
    for i in range(depth):
        g = norm_gains[i]
        j = i // 2
        xf = _ffn(xf, g[0], g[1], ffn1_w_in[i], ffn1_w_out[i])
        if i % 2 == 0:
            qkv = _qkv_proj(xf, g[2], da_w_qkv[j]).reshape(b, s, 3 * d)
            lambda_init = 0.8 - 0.6 * math.exp(-0.3 * i)
            attn = _diff_attention(qkv, da_lambda[j], da_subln[j], lambda_init)
            w_o = da_w_o[j]
        else:
            qkv = _qkv_proj(xf, g[2], sb_w_qkv[j]).reshape(b, s, 3 * d)
            attn = _stick_breaking_attention(qkv)
            w_o = sb_w_o[j]
        xf = _post_mixer(xf, attn.reshape(n, d), p[i].reshape(n, p.shape[-1]), g[3:8], w_o,
                         ffn2_w_in[i], ffn2_w_out[i], ple_w_proj[i], ple_w_gate[i])
    return xf.reshape(b, s, d)
```

```python
import functools
import math

import jax
import jax.numpy as jnp
from jax import lax
from jax.experimental import pallas as pl
from jax.experimental.pallas import tpu as pltpu

EPS = 1e-6
NEG_BIG = -1e30
LOG2E = math.log2(math.e)
CHUNK = 64
HEAD_LANES = 128
HALF = 64
QK_SCALE = 1.0 / math.sqrt(HALF)
ATT_BLOCK = 256
DA_STREAMS = 4
SB_STREAMS = 8
SB_UNIT_ROWS = 256
SB_MAX_LOG2 = 126.0
ROW_TILE = 512
FFN_ROW_TILE = 1024
FF_CHUNK = 256
QKV_CHUNK = 512
VMEM_LIMIT = 56 * 1024 * 1024

F32 = jnp.float32
BF16 = jnp.bfloat16


def _rms(x, g):
    ms = jnp.mean(x * x, axis=-1, keepdims=True)
    return x * lax.rsqrt(ms + EPS) * g


def _dot(a, b):
    return lax.dot_general(a, b, (((1,), (0,)), ((), ())), preferred_element_type=F32)


def _dot_nt(a, b):
    return lax.dot_general(a, b, (((1,), (1,)), ((), ())), preferred_element_type=F32)


def _resident(shape):
    zeros = (0,) * len(shape)
    return pl.BlockSpec(shape, lambda *_: zeros, pipeline_mode=pl.Buffered(1))


def _params(n_axes):
    return pltpu.CompilerParams(
        dimension_semantics=("arbitrary",) * n_axes, vmem_limit_bytes=VMEM_LIMIT)


def _swiglu_half_step(read_x, write_x, gpre, gpost, win_ref, wout_ref, h_ref, acc_ref,
                      before=None, after=None):
    half = h_ref.shape[0] // 2
    halves = [slice(0, half), slice(half, 2 * half)]
    f = wout_ref.shape[0]

    def pre(rows):
        if before is not None:
            before(rows)
        h_ref[rows, :] = _rms(read_x(rows), gpre).astype(BF16)
        acc_ref[rows, :] = jnp.zeros((half, acc_ref.shape[1]), F32)

    def post(rows):
        write_x(rows, read_x(rows) + 0.5 * _rms(acc_ref[rows, :], gpost))
        if after is not None:
            after(rows)

    def chunks(rows, overlapped):
        for j in range(f // FF_CHUNK):
            lo = j * FF_CHUNK
            h = h_ref[rows, :]
            a = _dot(h, win_ref[:, lo:lo + FF_CHUNK])
            b = _dot(h, win_ref[:, f + lo:f + lo + FF_CHUNK])
            g = (a * jax.nn.sigmoid(a) * b).astype(BF16)
            acc_ref[rows, :] += _dot(g, wout_ref[lo:lo + FF_CHUNK, :])
            if j == 0:
                overlapped()

    pre(halves[0])
    chunks(halves[0], lambda: pre(halves[1]))
    chunks(halves[1], lambda: post(halves[0]))
    post(halves[1])


def _ffn_kernel(x_ref, gpre_ref, gpost_ref, win_ref, wout_ref, o_ref, h_ref, acc_ref):
    def write(rows, value):
        o_ref[rows, :] = value

    _swiglu_half_step(lambda rows: x_ref[rows, :], write, gpre_ref[...], gpost_ref[...],
                      win_ref, wout_ref, h_ref, acc_ref)


def _ffn(x, g_pre, g_post, w_in, w_out):
    n, d = x.shape
    win = w_in.astype(BF16)
    wout = w_out.astype(BF16)
    row = pl.BlockSpec((FFN_ROW_TILE, d), lambda i: (i, 0))
    return pl.pallas_call(
        _ffn_kernel,
        grid=(n // FFN_ROW_TILE,),
        in_specs=[row, _resident((1, d)), _resident((1, d)),
                  _resident(win.shape), _resident(wout.shape)],
        out_specs=row,
        out_shape=jax.ShapeDtypeStruct((n, d), F32),
        scratch_shapes=[pltpu.VMEM((FFN_ROW_TILE, d), BF16), pltpu.VMEM((FFN_ROW_TILE, d), F32)],
        compiler_params=_params(1),
        name="ffn",
    )(x, g_pre.reshape(1, d), g_post.reshape(1, d), win, wout)


def _qkv_kernel(x_ref, g_ref, w_ref, o_ref):
    h = _rms(x_ref[...], g_ref[...]).astype(BF16)
    for c in range(w_ref.shape[1] // QKV_CHUNK):
        cols = slice(c * QKV_CHUNK, (c + 1) * QKV_CHUNK)
        o_ref[:, cols] = _dot(h, w_ref[:, cols]).astype(BF16)


def _qkv_proj(x, g, w_qkv):
    n, d = x.shape
    d3 = w_qkv.shape[1]
    return pl.pallas_call(
        _qkv_kernel,
        grid=(n // ROW_TILE,),
        in_specs=[pl.BlockSpec((ROW_TILE, d), lambda i: (i, 0)),
                  _resident((1, d)), _resident((d, d3))],
        out_specs=pl.BlockSpec((ROW_TILE, d3), lambda i: (i, 0)),
        out_shape=jax.ShapeDtypeStruct((n, d3), BF16),
        compiler_params=_params(1),
        name="qkv_proj",
    )(x, g.reshape(1, d), w_qkv.astype(BF16))


def _stack_halves(q):
    lane = lax.broadcasted_iota(jnp.int32, q.shape, 1)
    qs = q.astype(F32) * QK_SCALE
    lo = jnp.where(lane < HALF, qs, 0.0)
    hi = jnp.where(lane >= HALF, qs, 0.0)
    return jnp.concatenate([lo, hi], axis=0).astype(BF16)


def _attn_specs(d, seq, n_streams):
    width = n_streams * HEAD_LANES
    n_groups = d // width

    def spec(third):
        return pl.BlockSpec((1, seq, width), lambda b, h, *_: (b, 0, third * n_groups + h))
    return n_groups, spec


def _lanes(s):
    return slice(s * HEAD_LANES, (s + 1) * HEAD_LANES)


def _da_kernel(slope_ref, lamv_ref, subln_ref, q_ref, k_ref, v_ref, o_ref,
               q2_ref, vones_ref, s_ref, mx_ref, acc_ref, *, lambda_init):
    t = ATT_BLOCK
    n_blocks = q_ref.shape[1] // t
    streams = range(DA_STREAMS)
    slopes = [slope_ref[pl.program_id(1) * DA_STREAMS + s] for s in streams]

    for s in streams:
        vones_ref[s, :, :HEAD_LANES] = v_ref[0, :, _lanes(s)]
        vones_ref[s, :, HEAD_LANES:] = jnp.ones((v_ref.shape[1], HEAD_LANES), BF16)

    lv = lamv_ref[...]
    lam = (jnp.exp(jnp.sum(lv[0:1] * lv[1:2], axis=1, keepdims=True))
           - jnp.exp(jnp.sum(lv[2:3] * lv[3:4], axis=1, keepdims=True)) + lambda_init)
    subln = subln_ref[...]

    row = lax.broadcasted_iota(jnp.int32, (t, t), 0)
    col = lax.broadcasted_iota(jnp.int32, (t, t), 1)
    diag_shape = (row - jnp.abs(row - col)).astype(F32)
    allowed = col // CHUNK <= row // CHUNK
    diag_bias = [jnp.where(allowed, slopes[s] * diag_shape, NEG_BIG) for s in streams]
    col_f = lax.broadcasted_iota(jnp.int32, (1, t), 1).astype(F32)

    units = [(s, c) for s in streams for c in range(2)]
    n_lane_tiles = t // HEAD_LANES

    def lane_fold(x, op):
        out = x[:, :HEAD_LANES]
        for c in range(1, n_lane_tiles):
            out = op(out, x[:, c * HEAD_LANES:(c + 1) * HEAD_LANES])
        return out

    def key_rows(j):
        return pl.ds(pl.multiple_of(j * t, t), t)

    def score_pass(tiles, first):
        work = [(j, bias, u) for j, bias in tiles for u in range(len(units))]
        raw = {}

        def matmul(n):
            j, _, u = work[n]
            s, c = units[u]
            raw[n] = _dot_nt(q2_ref[s, c * t:(c + 1) * t, :], k_ref[0, key_rows(j), _lanes(s)])

        def finish(n):
            j, bias, u = work[n]
            sc = raw.pop(n) + bias[units[u][0]]
            s_ref[u, j] = sc
            tile_max = lane_fold(sc, jnp.maximum)
            mx_ref[u] = tile_max if first else jnp.maximum(mx_ref[u], tile_max)

        for step in range(len(work) + 1):
            if step < len(work):
                matmul(step)
            if step >= 1:
                finish(step - 1)

    def value_pass(tiles, first):
        for u, (s, c) in enumerate(units):
            m = jnp.concatenate([mx_ref[u]] * n_lane_tiles, axis=1)
            pv = None
            for j in tiles:
                p = jnp.exp(s_ref[u, j] - m)
                part = _dot(p, vones_ref[s, key_rows(j), :])
                pv = part if pv is None else pv + part
            if first:
                acc_ref[u] = pv
            else:
                acc_ref[u] += pv

    def finish_block(i):
        rows = pl.ds(pl.multiple_of(i * t, t), t)
        for s in streams:
            pv0, pv1 = acc_ref[2 * s], acc_ref[2 * s + 1]
            o = (pv0[:, :HEAD_LANES] / pv0[:, HEAD_LANES:]
                 - lam * (pv1[:, :HEAD_LANES] / pv1[:, HEAD_LANES:]))
            o = _rms(o, subln) * (1.0 - lambda_init)
            o_ref[0, rows, _lanes(s)] = o.astype(BF16)

    def q_block(i, carry):
        rows = pl.ds(pl.multiple_of(i * t, t), t)
        for s in streams:
            q2_ref[s] = _stack_halves(q_ref[0, rows, _lanes(s)])

        def earlier(j):
            back = jnp.asarray((i - j) * t, F32)
            return j, [slopes[s] * (col_f - back) for s in streams]

        score_pass([(i, diag_bias)], True)

        def score_pair(n, c):
            score_pass([earlier(2 * n), earlier(2 * n + 1)], False)
            return c

        lax.fori_loop(0, i // 2, score_pair, 0)

        @pl.when(i % 2 == 1)
        def _():
            score_pass([earlier(i - 1)], False)

        for u in range(len(units)):
            row_max = jnp.max(mx_ref[u], axis=1, keepdims=True)
            mx_ref[u] = jnp.broadcast_to(row_max, (t, HEAD_LANES))

        value_pass([0], True)

        def value_pair(n, c):
            value_pass([2 * n + 1, 2 * n + 2], False)
            return c

        lax.fori_loop(0, i // 2, value_pair, 0)

        @pl.when(i % 2 == 1)
        def _():
            value_pass([i], False)

        finish_block(i)
        return carry

    lax.fori_loop(0, n_blocks, q_block, 0)


def _diff_attention(qkv, lam_vecs, subln_g, lambda_init):
    b, s, d3 = qkv.shape
    d = d3 // 3
    nh = d // HEAD_LANES
    n_groups, spec = _attn_specs(d, s, DA_STREAMS)
    slopes = jnp.asarray([2.0 ** (-8.0 * (h + 1) / nh) for h in range(nh)], F32)
    grid_spec = pltpu.PrefetchScalarGridSpec(
        num_scalar_prefetch=1,
        grid=(b, n_groups),
        in_specs=[pl.BlockSpec(lam_vecs.shape, lambda *_: (0, 0)),
                  pl.BlockSpec((1, HEAD_LANES), lambda *_: (0, 0)),
                  spec(0), spec(1), spec(2)],
        out_specs=spec(0),
        scratch_shapes=[
            pltpu.VMEM((DA_STREAMS, 2 * ATT_BLOCK, HEAD_LANES), BF16),
            pltpu.VMEM((DA_STREAMS, s, 2 * HEAD_LANES), BF16),
            pltpu.VMEM((2 * DA_STREAMS, s // ATT_BLOCK, ATT_BLOCK, ATT_BLOCK), F32),
            pltpu.VMEM((2 * DA_STREAMS, ATT_BLOCK, HEAD_LANES), F32),
            pltpu.VMEM((2 * DA_STREAMS, ATT_BLOCK, 2 * HEAD_LANES), F32)],
    )
    return pl.pallas_call(
        functools.partial(_da_kernel, lambda_init=lambda_init),
        grid_spec=grid_spec,
        out_shape=jax.ShapeDtypeStruct((b, s, d), BF16),
        compiler_params=_params(2),
        name="diff_attention",
    )(slopes, lam_vecs, subln_g.reshape(1, HEAD_LANES), qkv, qkv, qkv)


def _sb_kernel(q_ref, k_ref, v_ref, o_ref, q2_ref, tail_ref, acc_ref):
    t = ATT_BLOCK
    n_blocks = q_ref.shape[1] // t
    streams = range(SB_STREAMS)
    cr = SB_UNIT_ROWS
    per_stream = 2 * t // cr
    units = [(s, c) for s in streams for c in range(per_stream)]
    n_units = len(units)

    row = lax.broadcasted_iota(jnp.int32, (2 * t, t), 0) % t
    col = lax.broadcasted_iota(jnp.int32, (2 * t, t), 1)
    strict = col < row
    strict_u = [strict[c * cr:(c + 1) * cr] for c in range(per_stream)]
    kr = lax.broadcasted_iota(jnp.int32, (t, t), 0)
    kc = lax.broadcasted_iota(jnp.int32, (t, t), 1)
    later = (kr > kc).astype(BF16)
    lane = lax.broadcasted_iota(jnp.int32, (t, HEAD_LANES), 1)

    def update(tiles, first):
        work = [(j, u) for j in tiles for u in range(n_units)]
        ws, mids = {}, {}

        def keys(n):
            return pl.ds(pl.multiple_of(work[n][0] * t, t), t)

        def stage_scores(n):
            s, c = units[work[n][1]]
            w = _dot_nt(q2_ref[s, c * cr:(c + 1) * cr, :], k_ref[0, keys(n), _lanes(s)]) * LOG2E
            ws[n] = w

        def stage_costs(n):
            w = ws.pop(n)
            cost = jnp.maximum(jnp.log2(1.0 + jnp.exp2(jnp.minimum(w, SB_MAX_LOG2))), w)
            log_beta = w - cost
            if first:
                cost = jnp.where(strict_u[units[work[n][1]][1]], cost, 0.0)
            sums = _dot(cost, later)
            total = jnp.broadcast_to(sums[:, 0:1] + cost[:, 0:1], (cr, HEAD_LANES))
            mids[n] = (log_beta - sums, total)

        def stage_values(n):
            u = work[n][1]
            s, c = units[u]
            pre, total = mids.pop(n)
            if first:
                a = jnp.where(strict_u[c], jnp.exp2(pre), 0.0)
                acc_ref[u] = _dot(a, v_ref[0, keys(n), _lanes(s)])
                tail_ref[u] = total
            else:
                tail = tail_ref[u]
                a = jnp.exp2(pre - jnp.concatenate([tail] * (t // HEAD_LANES), axis=1))
                acc_ref[u] += _dot(a, v_ref[0, keys(n), _lanes(s)])
                tail_ref[u] = tail + total

        for step in range(len(work) + 2):
            if step < len(work):
                stage_scores(step)
            if 0 <= step - 1 < len(work):
                stage_costs(step - 1)
            if 0 <= step - 2 < len(work):
                stage_values(step - 2)

    def q_block(i, carry):
        rows = pl.ds(pl.multiple_of(i * t, t), t)
        for s in streams:
            q2_ref[s] = _stack_halves(q_ref[0, rows, _lanes(s)])
        update([i], True)

        def k_pair(n, c):
            update([i - 1 - 2 * n, i - 2 - 2 * n], False)
            return c

        lax.fori_loop(0, i // 2, k_pair, 0)

        @pl.when(i % 2 == 1)
        def _():
            update([0], False)

        for s in streams:
            acc = jnp.concatenate([acc_ref[s * per_stream + c] for c in range(per_stream)], axis=0)
            o_ref[0, rows, _lanes(s)] = jnp.where(lane < HALF, acc[:t], acc[t:]).astype(BF16)
        return carry

    lax.fori_loop(0, n_blocks, q_block, 0)


def _stick_breaking_attention(qkv):
    b, s, d3 = qkv.shape
    d = d3 // 3
    n_groups, spec = _attn_specs(d, s, SB_STREAMS)
    n_units = SB_STREAMS * 2 * ATT_BLOCK // SB_UNIT_ROWS
    return pl.pallas_call(
        _sb_kernel,
        grid=(b, n_groups),
        in_specs=[spec(0), spec(1), spec(2)],
        out_specs=spec(0),
        out_shape=jax.ShapeDtypeStruct((b, s, d), BF16),
        scratch_shapes=[pltpu.VMEM((SB_STREAMS, 2 * ATT_BLOCK, HEAD_LANES), BF16),
                        pltpu.VMEM((n_units, SB_UNIT_ROWS, HEAD_LANES), F32),
                        pltpu.VMEM((n_units, SB_UNIT_ROWS, HEAD_LANES), F32)],
        compiler_params=_params(2),
        name="stick_breaking_attention",
    )(qkv, qkv, qkv)


def _post_mixer_kernel(x_ref, a_ref, p_ref, g_ref, wo_ref, win_ref, wout_ref, wproj_ref,
                       wgate_ref, o_ref, h_ref, acc_ref):
    def write(rows, value):
        o_ref[rows, :] = value

    def mixer_projection(rows):
        o_ref[rows, :] = x_ref[rows, :] + _rms(_dot(a_ref[rows, :], wo_ref[...]), g_ref[0:1])

    def layer_embedding(rows):
        x = o_ref[rows, :]
        e = _dot(p_ref[rows, :].astype(BF16), wproj_ref[...])
        gate = jax.nn.sigmoid(_dot(_rms(x, g_ref[3:4]).astype(BF16), wgate_ref[...]))
        o_ref[rows, :] = x + _rms(gate * e, g_ref[4:5])

    _swiglu_half_step(lambda rows: o_ref[rows, :], write, g_ref[1:2], g_ref[2:3],
                      win_ref, wout_ref, h_ref, acc_ref,
                      before=mixer_projection, after=layer_embedding)


def _post_mixer(x, attn, p, gains, w_o, w_in, w_out, w_proj, w_gate):
    n, d = x.shape
    dp = p.shape[1]
    row = pl.BlockSpec((FFN_ROW_TILE, d), lambda i: (i, 0))
    weights = [w.astype(BF16) for w in (w_o, w_in, w_out, w_proj, w_gate)]
    return pl.pallas_call(
        _post_mixer_kernel,
        grid=(n // FFN_ROW_TILE,),
        in_specs=[row, row, pl.BlockSpec((FFN_ROW_TILE, dp), lambda i: (i, 0)),
                  _resident(gains.shape)] + [_resident(w.shape) for w in weights],
        out_specs=row,
        out_shape=jax.ShapeDtypeStruct((n, d), F32),
        scratch_shapes=[pltpu.VMEM((FFN_ROW_TILE, d), BF16), pltpu.VMEM((FFN_ROW_TILE, d), F32)],
        compiler_params=_params(1),
        name="post_mixer",
    )(x, attn, p, gains, *weights)


def kernel(x, p, norm_gains, ffn1_w_in, ffn1_w_out, ffn2_w_in, ffn2_w_out, da_w_qkv, da_w_o,
           da_lambda, da_subln, sb_w_qkv, sb_w_o, ple_w_proj, ple_w_gate):
    b, s, d = x.shape
    depth = p.shape[0]
    n = b * s
    xf = x.reshape(n, d)
    for i in range(depth):
        g = norm_gains[i]
        j = i // 2
        xf = _ffn(xf, g[0], g[1], ffn1_w_in[i], ffn1_w_out[i])
        if i % 2 == 0:
            qkv = _qkv_proj(xf, g[2], da_w_qkv[j]).reshape(b, s, 3 * d)
            lambda_init = 0.8 - 0.6 * math.exp(-0.3 * i)
            attn = _diff_attention(qkv, da_lambda[j], da_subln[j], lambda_init)
            w_o = da_w_o[j]
        else:
            qkv = _qkv_proj(xf, g[2], sb_w_qkv[j]).reshape(b, s, 3 * d)
            attn = _stick_breaking_attention(qkv)
            w_o = sb_w_o[j]
        xf = _post_mixer(xf, attn.reshape(n, d), p[i].reshape(n, p.shape[-1]), g[3:8], w_o,
                         ffn2_w_in[i], ffn2_w_out[i], ple_w_proj[i], ple_w_gate[i])
    return xf.reshape(b, s, d)
```
